```python
import jax, jax.numpy as jnp
from jax import lax
import numpy as np

D_MODEL = 1024
BATCH = 1
SEQ = 16384
DEPTH = 4

HEAD_DIM = 64
N_Q_HEADS = 12
N_KV_HEADS_A = 4
N_MEM_HEADS = 4
N_MEM = 256
MIX_WIDTH = (N_Q_HEADS + N_MEM_HEADS) * HEAD_DIM
W_IN_A = (N_Q_HEADS + 2 * N_KV_HEADS_A + N_MEM_HEADS) * HEAD_DIM
W_IN_B = (3 * N_Q_HEADS + N_MEM_HEADS) * HEAD_DIM
WINDOW = 128
BLOCK = 128
GRID_W = 64
NA_ROWS_MAX = 8
NA_COLS = 16
D_FF = -(-8 * D_MODEL // (3 * 256)) * 256
N_LAYERS_A = (DEPTH + 1) // 2
N_LAYERS_B = DEPTH // 2
EPS = 1e-6
NEG_INF = -1e30

kernel_name = "hybrid_window_gqa_neighbourhood_memory_encoder"


def rmsnorm(x, g):
    x32 = x.astype(jnp.float32)
    y = x32 * lax.rsqrt(jnp.mean(x32 * x32, axis=-1, keepdims=True) + EPS)
    return (y * g.astype(jnp.float32)).astype(x.dtype)


def alibi_slopes(n_heads):
    return 2.0 ** (-8.0 * jnp.arange(1, n_heads + 1, dtype=jnp.float32) / n_heads)


def windowed_gqa(q, k, v, sink):
    B, S, H, D = q.shape
    Hk = k.shape[2]
    G = H // Hk
    nb = S // BLOCK
    qb = q.reshape(B, nb, BLOCK, Hk, G, D)
    pad = ((0, 0), (BLOCK, BLOCK), (0, 0), (0, 0))
    kp = jnp.pad(k, pad).reshape(B, nb + 2, BLOCK, Hk, D)
    vp = jnp.pad(v, pad).reshape(B, nb + 2, BLOCK, Hk, D)
    kband = jnp.concatenate([kp[:, :-2], kp[:, 1:-1], kp[:, 2:]], axis=2)
    vband = jnp.concatenate([vp[:, :-2], vp[:, 1:-1], vp[:, 2:]], axis=2)
    q_pos = jnp.arange(S).reshape(nb, BLOCK)
    k_pos = jnp.arange(nb)[:, None] * BLOCK - BLOCK + jnp.arange(3 * BLOCK)[None, :]
    dist = jnp.abs(q_pos[:, :, None] - k_pos[:, None, :])
    allowed = (dist <= WINDOW) & (k_pos[:, None, :] >= 0) & (k_pos[:, None, :] < S)
    slopes = alibi_slopes(H).reshape(1, Hk, G, 1, 1, 1)
    s = jnp.einsum('bnikgd,bnukd->bkgniu', qb, kband).astype(jnp.float32) * (D ** -0.5)
    s = s - slopes * dist.astype(jnp.float32)[None, None, None]
    s = jnp.where(allowed[None, None, None], s, NEG_INF)
    sink_col = jnp.broadcast_to(sink.astype(jnp.float32).reshape(1, Hk, G, 1, 1, 1), s.shape[:-1] + (1,))
    p = jax.nn.softmax(jnp.concatenate([s, sink_col], axis=-1), axis=-1)[..., :-1]
    o = jnp.einsum('bkgniu,bnukd->bnikgd', p.astype(v.dtype), vband)
    return o.reshape(B, S, H * D)


def neighbourhood_attn(q, k, v, rpb):
    B, S, H, D = q.shape
    rows = S // GRID_W
    kr = min(NA_ROWS_MAX, rows)
    qg = q.reshape(B, rows, GRID_W, H, D)
    kg = k.reshape(B, rows, GRID_W, H, D)
    vg = v.reshape(B, rows, GRID_W, H, D)
    r = jnp.arange(rows)
    row_start = jnp.clip(r - kr // 2, 0, rows - kr)
    row_idx = row_start[:, None] + jnp.arange(kr)[None, :]
    kband = kg[:, row_idx]
    vband = vg[:, row_idx]
    c = jnp.arange(GRID_W)
    col_start = jnp.clip(c - NA_COLS // 2, 0, GRID_W - NA_COLS)
    in_win = (c[None, :] >= col_start[:, None]) & (c[None, :] < col_start[:, None] + NA_COLS)
    dr_idx = row_idx - r[:, None] + NA_ROWS_MAX - 1
    dc_idx = jnp.clip(c[None, :] - c[:, None], -(NA_COLS - 1), NA_COLS - 1) + NA_COLS - 1
    bias = rpb.astype(jnp.float32)[:, dr_idx][:, :, :, dc_idx]
    bias = bias.transpose(0, 1, 3, 2, 4)
    s = jnp.einsum('brchd,brijhd->bhrcij', qg, kband).astype(jnp.float32) * (D ** -0.5)
    s = s + bias[None]
    s = jnp.where(in_win[None, None, None, :, None, :], s, NEG_INF)
    sh = s.shape
    p = jax.nn.softmax(s.reshape(sh[:-2] + (kr * GRID_W,)), axis=-1).reshape(sh)
    o = jnp.einsum('bhrcij,brijhd->brchd', p.astype(v.dtype), vband)
    return o.reshape(B, S, H * D)


def memory_attn(q, k, v):
    B, S, Hm, D = q.shape
    s = jnp.einsum('bshd,bmhd->bhsm', q, k).astype(jnp.float32) * (D ** -0.5)
    p = jax.nn.softmax(s, axis=-1)
    o = jnp.einsum('bhsm,bmhd->bshd', p.astype(v.dtype), v)
    return o.reshape(B, S, Hm * D)


def setup_inputs(seed: int = 0) -> dict:
    key = jax.random.key(seed)
    ks = jax.random.split(key, 16)
    f32 = jnp.float32
    nrm = lambda k, shape, scale: jax.random.normal(k, shape, f32) * scale
    return {
        "x": nrm(ks[0], (BATCH, SEQ, D_MODEL), 1.0),
        "mem": nrm(ks[1], (BATCH, N_MEM, D_MODEL), 1.0),
        "norm_mix": 1.0 + nrm(ks[2], (DEPTH, D_MODEL), 0.02),
        "norm_ffn": 1.0 + nrm(ks[3], (DEPTH, D_MODEL), 0.02),
        "norm_mem": 1.0 + nrm(ks[4], (D_MODEL,), 0.02),
        "norm_final": 1.0 + nrm(ks[5], (D_MODEL,), 0.02),
        "w_in_a": nrm(ks[6], (N_LAYERS_A, D_MODEL, W_IN_A), D_MODEL ** -0.5),
        "sink_a": nrm(ks[7], (N_LAYERS_A, N_Q_HEADS), 0.5),
        "w_in_b": nrm(ks[8], (N_LAYERS_B, D_MODEL, W_IN_B), D_MODEL ** -0.5),
        "rpb_b": nrm(ks[9], (N_LAYERS_B, N_Q_HEADS, 2 * NA_ROWS_MAX - 1, 2 * NA_COLS - 1), 0.3),
        "w_mem_kv": nrm(ks[10], (DEPTH, D_MODEL, 2 * N_MEM_HEADS * HEAD_DIM), D_MODEL ** -0.5),
        "w_out": nrm(ks[11], (DEPTH, MIX_WIDTH, D_MODEL), MIX_WIDTH ** -0.5),
        "w_gate_up": nrm(ks[12], (DEPTH, D_MODEL, 2 * D_FF), D_MODEL ** -0.5),
        "w_down": nrm(ks[13], (DEPTH, D_FF, D_MODEL), D_FF ** -0.5),
    }


def reference(x, mem, norm_mix, norm_ffn, norm_mem, norm_final, w_in_a, sink_a, w_in_b, rpb_b,
              w_mem_kv, w_out, w_gate_up, w_down):
    B, S, _ = x.shape
    qw = N_Q_HEADS * HEAD_DIM
    kvw = N_KV_HEADS_A * HEAD_DIM
    mw = N_MEM_HEADS * HEAD_DIM
    mem_n = rmsnorm(mem, norm_mem)
    for i in range(DEPTH):
        li = i // 2
        h = rmsnorm(x, norm_mix[i])
        mkv = mem_n @ w_mem_kv[i]
        mk = mkv[..., :mw].reshape(B, N_MEM, N_MEM_HEADS, HEAD_DIM)
        mv = mkv[..., mw:].reshape(B, N_MEM, N_MEM_HEADS, HEAD_DIM)
        if i % 2 == 0:
            proj = h @ w_in_a[li]
            q, k, v, qm = jnp.split(proj, [qw, qw + kvw, qw + 2 * kvw], axis=-1)
            y_tok = windowed_gqa(q.reshape(B, S, N_Q_HEADS, HEAD_DIM),
                                 k.reshape(B, S, N_KV_HEADS_A, HEAD_DIM),
                                 v.reshape(B, S, N_KV_HEADS_A, HEAD_DIM), sink_a[li])
        else:
            proj = h @ w_in_b[li]
            q, k, v, qm = jnp.split(proj, [qw, 2 * qw, 3 * qw], axis=-1)
            y_tok = neighbourhood_attn(q.reshape(B, S, N_Q_HEADS, HEAD_DIM),
                                       k.reshape(B, S, N_Q_HEADS, HEAD_DIM),
                                       v.reshape(B, S, N_Q_HEADS, HEAD_DIM), rpb_b[li])
        y_mem = memory_attn(qm.reshape(B, S, N_MEM_HEADS, HEAD_DIM), mk, mv)
        x = x + jnp.concatenate([y_tok, y_mem], axis=-1) @ w_out[i]
        h = rmsnorm(x, norm_ffn[i])
        gu = h @ w_gate_up[i]
        x = x + (jax.nn.silu(gu[..., :D_FF]) * gu[..., D_FF:]) @ w_down[i]
    return rmsnorm(x, norm_final)
```

```python
import functools

import jax
import jax.numpy as jnp
from jax import lax
from jax.experimental import pallas as pl
from jax.experimental.pallas import tpu as pltpu

F32 = jnp.float32
BF16 = jnp.bfloat16

HEAD_DIM = 64
N_Q_HEADS = 12
N_KV_HEADS_A = 4
N_MEM_HEADS = 4
WINDOW = 128
BLOCK_A = 128
GRID_W = 64
NA_ROWS = 8
NA_COLS = 16
ROWS_PER_STEP_B = 8
EPS = 1e-6
NEG_INF = -1e30
SCALE = HEAD_DIM ** -0.5

QW = N_Q_HEADS * HEAD_DIM
KVW_A = N_KV_HEADS_A * HEAD_DIM
MW = N_MEM_HEADS * HEAD_DIM

VMEM_LIMIT_BYTES = 56 * 1024 * 1024
ROW_TILE = 256

_NT = (((1,), (1,)), ((), ()))
_TN = (((0,), (0,)), ((), ()))


def _resident(shape):
    n = len(shape)
    return pl.BlockSpec(shape, lambda *_: (0,) * n)


def _rms(x, g):
    return x * lax.rsqrt(jnp.mean(x * x, axis=-1, keepdims=True) + EPS) * g


def _mem_kv_kernel(mem_ref, g_ref, w_ref, o_ref):
    mem_n = _rms(mem_ref[...], g_ref[...])
    o_ref[...] = jnp.dot(mem_n.astype(BF16), w_ref[...],
                         preferred_element_type=F32).astype(BF16)


def _mem_kv(mem, norm_mem, w_mem_kv):
    depth, d, w = w_mem_kv.shape
    n_mem = mem.shape[0]
    return pl.pallas_call(
        _mem_kv_kernel,
        grid=(depth,),
        in_specs=[_resident((n_mem, d)), _resident((1, d)),
                  pl.BlockSpec((None, d, w), lambda i: (i, 0, 0))],
        out_specs=pl.BlockSpec((None, n_mem, w), lambda i: (i, 0, 0)),
        out_shape=jax.ShapeDtypeStruct((depth, n_mem, w), BF16),
        name="mem_kv",
    )(mem, norm_mem.reshape(1, d), w_mem_kv)


def _mem_attn(qm, mkv):
    outs = []
    for h in range(N_MEM_HEADS):
        lo = h * HEAD_DIM
        s = lax.dot_general(qm[:, lo:lo + HEAD_DIM], mkv[:, lo:lo + HEAD_DIM], _NT,
                            preferred_element_type=F32)
        m = jnp.max(s, axis=-1, keepdims=True)
        e = jnp.exp(s - m)
        l = jnp.sum(e, axis=-1, keepdims=True)
        o = jnp.dot(e.astype(BF16), mkv[:, MW + lo:MW + lo + HEAD_DIM],
                    preferred_element_type=F32)
        outs.append(o / l)
    return outs


def _proj_kernel(x_ref, g_ref, w_ref, o_ref):
    h = _rms(x_ref[...], g_ref[...]).astype(BF16)
    o_ref[...] = jnp.dot(h, w_ref[...], preferred_element_type=F32).astype(BF16)


def _proj(x, g, w):
    s, d = x.shape
    n = w.shape[1]
    return pl.pallas_call(
        _proj_kernel,
        grid=(s // ROW_TILE,),
        in_specs=[pl.BlockSpec((ROW_TILE, d), lambda i: (i, 0)),
                  _resident((1, d)), _resident((d, n))],
        out_specs=pl.BlockSpec((ROW_TILE, n), lambda i: (i, 0)),
        out_shape=jax.ShapeDtypeStruct((s, n), BF16),
        compiler_params=pltpu.CompilerParams(
            dimension_semantics=("arbitrary",), vmem_limit_bytes=VMEM_LIMIT_BYTES),
        name="in_proj",
    )(x, g.reshape(1, d), w)


def _alibi_slope(h):
    return 2.0 ** (-8.0 * (h + 1) / N_Q_HEADS)


def _band_start(i, seq):
    return jnp.clip(i - 1, 0, seq // BLOCK_A - 3) * BLOCK_A


def _attn_a_kernel(sink_ref, cur_ref, kv_ref, mkv_ref, y_ref, *, seq):
    i = pl.program_id(0)
    band = 3 * BLOCK_A
    cur = cur_ref[...]
    q = cur[:, :QW] * SCALE
    qm = cur[:, QW + 2 * KVW_A:] * SCALE
    kv = kv_ref[...]

    delta = i * BLOCK_A - _band_start(i, seq)
    r = lax.broadcasted_iota(jnp.int32, (BLOCK_A, band), 0)
    c = lax.broadcasted_iota(jnp.int32, (BLOCK_A, band), 1)
    dist = jnp.abs(r - c + delta)
    allowed = dist <= WINDOW
    distf = dist.astype(F32)

    outs = []
    for h in range(N_Q_HEADS):
        j = h // (N_Q_HEADS // N_KV_HEADS_A)
        qh = q[:, h * HEAD_DIM:(h + 1) * HEAD_DIM]
        kj = kv[:, j * HEAD_DIM:(j + 1) * HEAD_DIM]
        vj = kv[:, KVW_A + j * HEAD_DIM:KVW_A + (j + 1) * HEAD_DIM]
        s = lax.dot_general(qh, kj, _NT, preferred_element_type=F32)
        s = s - _alibi_slope(h) * distf
        s = jnp.where(allowed, s, NEG_INF)
        sink = sink_ref[h]
        m = jnp.maximum(jnp.max(s, axis=-1, keepdims=True), sink)
        e = jnp.exp(s - m)
        l = jnp.sum(e, axis=-1, keepdims=True) + jnp.exp(sink - m)
        o = jnp.dot(e.astype(BF16), vj, preferred_element_type=F32)
        outs.append(o / l)
    outs += _mem_attn(qm, mkv_ref[...])
    y_ref[...] = jnp.concatenate(outs, axis=-1).astype(BF16)


def _attn_a(proj, mkv, sink):
    seq, w = proj.shape
    band = 3 * BLOCK_A
    return pl.pallas_call(
        functools.partial(_attn_a_kernel, seq=seq),
        grid=(seq // BLOCK_A,),
        in_specs=[
            pl.BlockSpec(memory_space=pltpu.SMEM),
            pl.BlockSpec((BLOCK_A, w), lambda i: (i, 0)),
            pl.BlockSpec((pl.Element(band), pl.Element(2 * KVW_A)),
                         lambda i: (_band_start(i, seq), QW)),
            _resident(mkv.shape),
        ],
        out_specs=pl.BlockSpec((BLOCK_A, QW + MW), lambda i: (i, 0)),
        out_shape=jax.ShapeDtypeStruct((seq, QW + MW), BF16),
        compiler_params=pltpu.CompilerParams(
            dimension_semantics=("arbitrary",), vmem_limit_bytes=VMEM_LIMIT_BYTES),
        name="attn_a",
    )(sink, proj, proj, mkv)


N_DR = 2 * NA_ROWS - 1
N_DC = 2 * NA_COLS - 1


def _bias_table_kernel(rpb_ref, o_ref):
    li = pl.program_id(0)
    hp = pl.program_id(1)
    kc = lax.broadcasted_iota(jnp.int32, (GRID_W, 2 * GRID_W), 0)
    lane = lax.broadcasted_iota(jnp.int32, (GRID_W, 2 * GRID_W), 1)
    first = lane < GRID_W
    qc = jnp.where(first, lane, lane - GRID_W)
    col_start = jnp.clip(qc - NA_COLS // 2, 0, GRID_W - NA_COLS)
    in_win = (kc >= col_start) & (kc < col_start + NA_COLS)
    idx = jnp.clip(kc - qc, -(NA_COLS - 1), NA_COLS - 1) + NA_COLS - 1
    base = (li * N_Q_HEADS + 2 * hp) * (N_DR * N_DC)

    def body(d, carry):
        acc = jnp.zeros((GRID_W, 2 * GRID_W), F32)
        for v in range(N_DC):
            a0 = rpb_ref[base + d * N_DC + v]
            a1 = rpb_ref[base + N_DR * N_DC + d * N_DC + v]
            acc = jnp.where(idx == v, jnp.where(first, a0, a1), acc)
        o_ref[pl.ds(pl.multiple_of(d * GRID_W, GRID_W), GRID_W), :] = jnp.where(in_win, acc, NEG_INF)
        return carry

    lax.fori_loop(0, N_DR, body, 0)


def _bias_tables(rpb_b):
    n_layers = rpb_b.shape[0]
    n_pairs = N_Q_HEADS // 2
    return pl.pallas_call(
        _bias_table_kernel,
        grid=(n_layers, n_pairs),
        in_specs=[pl.BlockSpec(memory_space=pltpu.SMEM)],
        out_specs=pl.BlockSpec((None, None, N_DR * GRID_W, 2 * GRID_W),
                               lambda li, hp: (li, hp, 0, 0)),
        out_shape=jax.ShapeDtypeStruct((n_layers, n_pairs, N_DR * GRID_W, 2 * GRID_W), F32),
        name="rpb_table",
    )(rpb_b.reshape(-1))


def _slab_start_row(b, rows):
    return jnp.clip(b * ROWS_PER_STEP_B - NA_ROWS // 2, 0, rows - 2 * ROWS_PER_STEP_B)


def _attn_b_kernel(q_ref, qm_ref, kv_ref, bias_ref, mkv_ref, y_ref, *, rows):
    b = pl.program_id(0)
    win = NA_ROWS * GRID_W
    lane = lax.broadcasted_iota(jnp.int32, (GRID_W, 2 * GRID_W), 1)
    first = lane < GRID_W
    slab0 = _slab_start_row(b, rows)

    def row_body(j, carry):
        r = b * ROWS_PER_STEP_B + j
        row_start = jnp.clip(r - NA_ROWS // 2, 0, rows - NA_ROWS)
        koff = pl.multiple_of((row_start - slab0) * GRID_W, GRID_W)
        boff = pl.multiple_of((row_start - r + NA_ROWS - 1) * GRID_W, GRID_W)
        qoff = pl.multiple_of(j * GRID_W, GRID_W)
        for hp in range(N_Q_HEADS // 2):
            lo = hp * 2 * HEAD_DIM
            qp = q_ref[pl.ds(qoff, GRID_W), lo:lo + 2 * HEAD_DIM] * SCALE
            zero = jnp.zeros_like(qp)
            qd = jnp.concatenate([jnp.where(first, qp, zero), jnp.where(first, zero, qp)], axis=0)
            kp = kv_ref[pl.ds(koff, win), lo:lo + 2 * HEAD_DIM]
            vp = kv_ref[pl.ds(koff, win), QW + lo:QW + lo + 2 * HEAD_DIM]
            st = lax.dot_general(kp, qd, _NT, preferred_element_type=F32)
            st = st + bias_ref[hp, pl.ds(boff, win), :]
            m = jnp.max(st, axis=0, keepdims=True)
            e = jnp.exp(st - m)
            l = jnp.sum(e, axis=0, keepdims=True)
            p = (e * (1.0 / l)).astype(BF16)
            o2 = lax.dot_general(p, vp, _TN, preferred_element_type=F32)
            yp = jnp.where(first, o2[:GRID_W], o2[GRID_W:])
            y_ref[pl.ds(qoff, GRID_W), lo:lo + 2 * HEAD_DIM] = yp.astype(BF16)
        return carry

    lax.fori_loop(0, ROWS_PER_STEP_B, row_body, 0)
    ym = _mem_attn(qm_ref[...] * SCALE, mkv_ref[...])
    y_ref[:, QW:] = jnp.concatenate(ym, axis=-1).astype(BF16)


def _attn_b(proj, mkv, bias):
    seq, w = proj.shape
    rows = seq // GRID_W
    tq = ROWS_PER_STEP_B * GRID_W
    slab = 2 * ROWS_PER_STEP_B * GRID_W
    return pl.pallas_call(
        functools.partial(_attn_b_kernel, rows=rows),
        grid=(rows // ROWS_PER_STEP_B,),
        in_specs=[
            pl.BlockSpec((tq, QW), lambda b: (b, 0)),
            pl.BlockSpec((tq, MW), lambda b: (b, 3 * QW // MW)),
            pl.BlockSpec((pl.Element(slab), pl.Element(2 * QW)),
                         lambda b: (_slab_start_row(b, rows) * GRID_W, QW)),
            _resident(bias.shape),
            _resident(mkv.shape),
        ],
        out_specs=pl.BlockSpec((tq, QW + MW), lambda b: (b, 0)),
        out_shape=jax.ShapeDtypeStruct((seq, QW + MW), BF16),
        compiler_params=pltpu.CompilerParams(
            dimension_semantics=("arbitrary",), vmem_limit_bytes=VMEM_LIMIT_BYTES),
        name="attn_b",
    )(proj, proj, proj, bias, mkv)


def _post_kernel(x_ref, y_ref, wo_ref, gf_ref, wgu_ref, wd_ref, gn_ref, *rest, d_ff, last):
    x1 = x_ref[...] + jnp.dot(y_ref[...], wo_ref[...], preferred_element_type=F32)
    h = _rms(x1, gf_ref[...]).astype(BF16)
    gu = jnp.dot(h, wgu_ref[...], preferred_element_type=F32)
    g = gu[:, :d_ff]
    u = gu[:, d_ff:]
    a = (g * (1.0 / (1.0 + jnp.exp(-g))) * u).astype(BF16)
    x2 = x1 + jnp.dot(a, wd_ref[...], preferred_element_type=F32)
    if last:
        (o_ref,) = rest
        o_ref[...] = _rms(x2, gn_ref[...])
    else:
        wn_ref, o_ref, p_ref = rest
        o_ref[...] = x2
        hn = _rms(x2, gn_ref[...]).astype(BF16)
        p_ref[...] = jnp.dot(hn, wn_ref[...], preferred_element_type=F32).astype(BF16)


def _post(x, y, w_out, g_ffn, w_gu, w_d, g_next, w_next):
    s, d = x.shape
    d_ff = w_d.shape[0]
    last = w_next is None
    row = lambda i: (i, 0)
    in_specs = [pl.BlockSpec((ROW_TILE, d), row), pl.BlockSpec((ROW_TILE, y.shape[1]), row),
                _resident(w_out.shape), _resident((1, d)), _resident(w_gu.shape),
                _resident(w_d.shape), _resident((1, d))]
    args = [x, y, w_out, g_ffn.reshape(1, d), w_gu, w_d, g_next.reshape(1, d)]
    out_specs = [pl.BlockSpec((ROW_TILE, d), row)]
    out_shape = [jax.ShapeDtypeStruct((s, d), F32)]
    if not last:
        n = w_next.shape[1]
        in_specs.append(_resident(w_next.shape))
        args.append(w_next)
        out_specs.append(pl.BlockSpec((ROW_TILE, n), row))
        out_shape.append(jax.ShapeDtypeStruct((s, n), BF16))
    outs = pl.pallas_call(
        functools.partial(_post_kernel, d_ff=d_ff, last=last),
        grid=(s // ROW_TILE,),
        in_specs=in_specs,
        out_specs=out_specs,
        out_shape=out_shape,
        compiler_params=pltpu.CompilerParams(
            dimension_semantics=("arbitrary",), vmem_limit_bytes=VMEM_LIMIT_BYTES),
        name="post_last" if last else "post",
    )(*args)
    return (outs[0], None) if last else (outs[0], outs[1])


def kernel(x, mem, norm_mix, norm_ffn, norm_mem, norm_final, w_in_a, sink_a, w_in_b, rpb_b,
           w_mem_kv, w_out, w_gate_up, w_down):
    batch, seq, d = x.shape
    assert batch == 1 and mem.shape[0] == 1
    assert seq % (ROWS_PER_STEP_B * GRID_W) == 0 and seq % ROW_TILE == 0
    depth = w_out.shape[0]
    xs = x.reshape(seq, d)
    w_in = [(w_in_a if i % 2 == 0 else w_in_b)[i // 2].astype(BF16) for i in range(depth)]
    w_out_b = w_out.astype(BF16)
    w_gu_b = w_gate_up.astype(BF16)
    w_d_b = w_down.astype(BF16)

    mkv = _mem_kv(mem.reshape(mem.shape[1], d), norm_mem, w_mem_kv.astype(BF16))
    bias = _bias_tables(rpb_b)

    proj = _proj(xs, norm_mix[0], w_in[0])
    for i in range(depth):
        if i % 2 == 0:
            y = _attn_a(proj, mkv[i], sink_a[i // 2])
        else:
            y = _attn_b(proj, mkv[i], bias[i // 2])
        last = i == depth - 1
        xs, proj = _post(xs, y, w_out_b[i], norm_ffn[i], w_gu_b[i], w_d_b[i],
                         norm_final if last else norm_mix[i + 1],
                         None if last else w_in[i + 1])
    return xs.reshape(batch, seq, d)
```

```python
import functools

import numpy as np

import jax
import jax.numpy as jnp
from jax import lax
from jax.experimental import pallas as pl
from jax.experimental.pallas import tpu as pltpu

F32 = jnp.float32
BF16 = jnp.bfloat16

HEAD_DIM = 64
N_Q_HEADS = 12
N_KV_HEADS_A = 4
GROUP_A = N_Q_HEADS // N_KV_HEADS_A
N_MEM_HEADS = 4
WINDOW = 128
BLOCK_A = 128
SUB_BLOCKS_A = 4
GRID_W = 64
NA_ROWS = 8
NA_COLS = 16
ROWS_PER_STEP_B = 8
EPS = 1e-6
NEG_INF = -1e30
SCALE = HEAD_DIM ** -0.5

LANES = 128
QW = N_Q_HEADS * HEAD_DIM
KVW_A = N_KV_HEADS_A * HEAD_DIM
MW = N_MEM_HEADS * HEAD_DIM
QUAD = 4 * HEAD_DIM

VMEM_LIMIT_BYTES = 56 * 1024 * 1024
ROW_TILE = 256

_NT = (((1,), (1,)), ((), ()))
_TN = (((0,), (0,)), ((), ()))


def _resident(shape):
    n = len(shape)
    return pl.BlockSpec(shape, lambda *_: (0,) * n)


def _rms(x, g):
    return x * lax.rsqrt(jnp.mean(x * x, axis=-1, keepdims=True) + EPS) * g


def _head_slot(shape):
    return lax.broadcasted_iota(jnp.int32, shape, 1) // HEAD_DIM


def _slot_masks(rows):
    slot = _head_slot((rows, QUAD))
    return [(slot == j).astype(BF16) for j in range(QUAD // HEAD_DIM)]


def _run_pipelined(units):
    state = None
    prev_finish = None
    for start, finish in units:
        nxt = start()
        if prev_finish is not None:
            prev_finish(state)
        state, prev_finish = nxt, finish
    prev_finish(state)


def _pair_unit(k_all, q2, vt2, second_row, sinks, biases, store):
    def start():
        return lax.dot_general(k_all(), q2(), _NT, preferred_element_type=F32)

    def finish(st):
        es, ms = [], []
        for p in range(2):
            sth = st[:, p * LANES:(p + 1) * LANES]
            if biases[p] is not None:
                sth = sth + biases[p]()
            m = jnp.max(sth, axis=0, keepdims=True)
            if sinks[p] is not None:
                m = jnp.maximum(m, sinks[p])
            es.append(jnp.exp(sth - m).astype(BF16))
            ms.append(m)
        v = vt2()
        n_v = v.shape[0]
        lhs = jnp.concatenate([v, jnp.ones((16, v.shape[1]), BF16)], axis=0)
        ot = jnp.dot(lhs, jnp.concatenate(es, axis=1), preferred_element_type=F32)
        outs = []
        for p in range(2):
            l = ot[n_v:n_v + 1, p * LANES:(p + 1) * LANES]
            if sinks[p] is not None:
                l = l + jnp.exp(sinks[p] - ms[p])
            r0 = p * second_row
            outs.append(ot[r0:r0 + HEAD_DIM, p * LANES:(p + 1) * LANES] * (1.0 / l))
        store(jnp.concatenate(outs, axis=0))

    return start, finish


def _mem_units(qm_ref, rows, mkv_ref, mvt_ref, masks, store_pair):
    def q2(t):
        qm = qm_ref[rows, :] * SCALE
        return jnp.concatenate([qm * masks[2 * t], qm * masks[2 * t + 1]], axis=0)

    return [_pair_unit(lambda: mkv_ref[:, :MW], functools.partial(q2, t),
                       lambda t=t: mvt_ref[t * LANES:(t + 1) * LANES, :], HEAD_DIM,
                       (None, None), (None, None), functools.partial(store_pair, t))
            for t in range(N_MEM_HEADS // 2)]


def _mem_kv_kernel(mem_ref, g_ref, w_ref, o_ref):
    mem_n = _rms(mem_ref[...], g_ref[...])
    o_ref[...] = jnp.dot(mem_n.astype(BF16), w_ref[...],
                         preferred_element_type=F32).astype(BF16)


def _mem_kv(mem, norm_mem, w_mem_kv):
    depth, d, w = w_mem_kv.shape
    n_mem = mem.shape[0]
    return pl.pallas_call(
        _mem_kv_kernel,
        grid=(depth,),
        in_specs=[_resident((n_mem, d)), _resident((1, d)),
                  pl.BlockSpec((None, d, w), lambda i: (i, 0, 0))],
        out_specs=pl.BlockSpec((None, n_mem, w), lambda i: (i, 0, 0)),
        out_shape=jax.ShapeDtypeStruct((depth, n_mem, w), BF16),
        name="mem_kv",
    )(mem, norm_mem.reshape(1, d), w_mem_kv)


def _proj_kernel(x_ref, g_ref, w_ref, o_ref):
    h = _rms(x_ref[...], g_ref[...]).astype(BF16)
    o_ref[...] = jnp.dot(h, w_ref[...], preferred_element_type=F32).astype(BF16)


def _proj(x, g, w):
    s, d = x.shape
    n = w.shape[1]
    return pl.pallas_call(
        _proj_kernel,
        grid=(s // ROW_TILE,),
        in_specs=[pl.BlockSpec((ROW_TILE, d), lambda i: (i, 0)),
                  _resident((1, d)), _resident((d, n))],
        out_specs=pl.BlockSpec((ROW_TILE, n), lambda i: (i, 0)),
        out_shape=jax.ShapeDtypeStruct((s, n), BF16),
        compiler_params=pltpu.CompilerParams(
            dimension_semantics=("arbitrary",), vmem_limit_bytes=VMEM_LIMIT_BYTES),
        name="in_proj",
    )(x, g.reshape(1, d), w)


def _alibi_slope(h):
    return 2.0 ** (-8.0 * (h + 1) / N_Q_HEADS)


def _q_column_order_a():
    cols = np.empty(QW, np.int32)
    for j in range(N_KV_HEADS_A):
        for g in range(GROUP_A):
            dst = QUAD * g + HEAD_DIM * j
            src = HEAD_DIM * (GROUP_A * j + g)
            cols[dst:dst + HEAD_DIM] = np.arange(src, src + HEAD_DIM)
    return cols


def _slab_block_a(i, nb):
    return jnp.clip(i * SUB_BLOCKS_A - 1, 0, nb - (SUB_BLOCKS_A + 2))


def _attn_a_kernel(sink_ref, q_ref, qm_ref, kv_ref, mkv_ref, y_ref, bias_ref, mvt_ref, vbt_ref,
                   *, nb):
    i = pl.program_id(0)
    band = 3 * BLOCK_A

    @pl.when(i == 0)
    def _():
        c = lax.broadcasted_iota(jnp.int32, (band, BLOCK_A), 0)
        r = lax.broadcasted_iota(jnp.int32, (band, BLOCK_A), 1)
        for var in range(3):
            dist = jnp.abs(r - c + var * BLOCK_A)
            allowed = dist <= WINDOW
            distf = dist.astype(F32)
            for h in range(N_Q_HEADS):
                bias_ref[var, h] = jnp.where(allowed, -(_alibi_slope(h) * distf), NEG_INF)

        mvt_ref[...] = mkv_ref[:, MW:].T

    slab_blk = _slab_block_a(i, nb)
    masks = _slot_masks(BLOCK_A)

    def store_pair(rows, t, z):
        y_ref[rows, t * LANES:(t + 1) * LANES] = z.T.astype(BF16)

    def token_unit(s, t, off, var):
        rows = slice(s * BLOCK_A, (s + 1) * BLOCK_A)
        heads = (2 * t, 2 * t + 1)
        kvh = [h // GROUP_A for h in heads]

        def k_all():
            if t == 0:
                vbt_ref[s] = kv_ref[pl.ds(off, band), KVW_A:].T
            return kv_ref[pl.ds(off, band), :KVW_A]

        def q2():
            return jnp.concatenate(
                [q_ref[rows, (h % GROUP_A) * QUAD:(h % GROUP_A + 1) * QUAD] * SCALE * masks[j]
                 for h, j in zip(heads, kvh)], axis=0)

        return _pair_unit(
            k_all, q2, lambda: vbt_ref[s, kvh[0] * HEAD_DIM:(kvh[1] + 1) * HEAD_DIM, :],
            (kvh[1] - kvh[0]) * HEAD_DIM, [sink_ref[h] for h in heads],
            [lambda h=h: bias_ref[var, h] for h in heads],
            functools.partial(store_pair, rows, t))

    units = []
    for s in range(SUB_BLOCKS_A):
        g_blk = i * SUB_BLOCKS_A + s
        band_blk = jnp.clip(g_blk - 1, 0, nb - 3)
        off = pl.multiple_of((band_blk - slab_blk) * BLOCK_A, BLOCK_A)
        var = g_blk - band_blk
        rows = slice(s * BLOCK_A, (s + 1) * BLOCK_A)
        units += [token_unit(s, t, off, var) for t in range(N_Q_HEADS // 2)]
        units += _mem_units(qm_ref, rows, mkv_ref, mvt_ref, masks,
                            lambda t, z, rows=rows: store_pair(rows, N_Q_HEADS // 2 + t, z))
    _run_pipelined(units)


def _attn_a(proj, mkv, sink):
    seq, _ = proj.shape
    nb = seq // BLOCK_A
    tq = SUB_BLOCKS_A * BLOCK_A
    slab = (SUB_BLOCKS_A + 2) * BLOCK_A
    return pl.pallas_call(
        functools.partial(_attn_a_kernel, nb=nb),
        grid=(nb // SUB_BLOCKS_A,),
        in_specs=[
            pl.BlockSpec(memory_space=pltpu.SMEM),
            pl.BlockSpec((tq, QW), lambda i: (i, 0)),
            pl.BlockSpec((tq, MW), lambda i: (i, (QW + 2 * KVW_A) // MW)),
            pl.BlockSpec((pl.Element(slab), pl.Element(2 * KVW_A)),
                         lambda i: (_slab_block_a(i, nb) * BLOCK_A, QW)),
            _resident(mkv.shape),
        ],
        out_specs=pl.BlockSpec((tq, QW + MW), lambda i: (i, 0)),
        out_shape=jax.ShapeDtypeStruct((seq, QW + MW), BF16),
        scratch_shapes=[pltpu.VMEM((3, N_Q_HEADS, 3 * BLOCK_A, BLOCK_A), F32),
                        pltpu.VMEM((MW, mkv.shape[0]), BF16),
                        pltpu.VMEM((SUB_BLOCKS_A, KVW_A, 3 * BLOCK_A), BF16)],
        compiler_params=pltpu.CompilerParams(
            dimension_semantics=("arbitrary",), vmem_limit_bytes=VMEM_LIMIT_BYTES),
        name="attn_a",
    )(sink, proj, proj, proj, mkv)


N_DR = 2 * NA_ROWS - 1
N_DC = 2 * NA_COLS - 1
N_QUADS = N_Q_HEADS // 4


def _bias_table_kernel(rpb_ref, o_ref):
    li = pl.program_id(0)
    hq = pl.program_id(1)
    kc = lax.broadcasted_iota(jnp.int32, (GRID_W, QUAD), 0)
    lane = lax.broadcasted_iota(jnp.int32, (GRID_W, QUAD), 1)
    qc = lane % GRID_W
    col_start = jnp.clip(qc - NA_COLS // 2, 0, GRID_W - NA_COLS)
    in_win = (kc >= col_start) & (kc < col_start + NA_COLS)
    idx = jnp.clip(kc - qc, -(NA_COLS - 1), NA_COLS - 1) + NA_COLS - 1
    slot_row = _head_slot((1, QUAD))
    base = (li * N_Q_HEADS + 4 * hq) * (N_DR * N_DC)

    def body(d, carry):
        acc = jnp.zeros((GRID_W, QUAD), F32)
        for v in range(N_DC):
            a = [rpb_ref[base + hl * (N_DR * N_DC) + d * N_DC + v] for hl in range(4)]
            val = jnp.where(slot_row == 0, a[0],
                            jnp.where(slot_row == 1, a[1], jnp.where(slot_row == 2, a[2], a[3])))
            acc = jnp.where(idx == v, val, acc)
        o_ref[pl.ds(pl.multiple_of(d * GRID_W, GRID_W), GRID_W), :] = jnp.where(in_win, acc, NEG_INF)
        return carry

    lax.fori_loop(0, N_DR, body, 0)


def _bias_tables(rpb_b):
    n_layers = rpb_b.shape[0]
    return pl.pallas_call(
        _bias_table_kernel,
        grid=(n_layers, N_QUADS),
        in_specs=[pl.BlockSpec(memory_space=pltpu.SMEM)],
        out_specs=pl.BlockSpec((None, None, N_DR * GRID_W, QUAD),
                               lambda li, hq: (li, hq, 0, 0)),
        out_shape=jax.ShapeDtypeStruct((n_layers, N_QUADS, N_DR * GRID_W, QUAD), F32),
        name="rpb_table",
    )(rpb_b.reshape(-1))


def _slab_start_row(b, rows):
    return jnp.clip(b * ROWS_PER_STEP_B - NA_ROWS // 2, 0, rows - 2 * ROWS_PER_STEP_B)


def _attn_b_kernel(q_ref, qm_ref, kv_ref, bias_ref, mkv_ref, y_ref, mvt_ref, *, rows):
    b = pl.program_id(0)
    win = NA_ROWS * GRID_W
    slot = _head_slot((GRID_W, QUAD))
    slab0 = _slab_start_row(b, rows)

    @pl.when(b == 0)
    def _():
        mvt_ref[...] = mkv_ref[:, MW:].T

    masks64 = _slot_masks(GRID_W)
    masks128 = _slot_masks(LANES)

    def quad_unit(j, hq):
        r = b * ROWS_PER_STEP_B + j
        row_start = jnp.clip(r - NA_ROWS // 2, 0, rows - NA_ROWS)
        koff = pl.multiple_of((row_start - slab0) * GRID_W, GRID_W)
        boff = pl.multiple_of((row_start - r + NA_ROWS - 1) * GRID_W, GRID_W)
        qrows = slice(j * GRID_W, (j + 1) * GRID_W)
        lo = hq * QUAD

        def start():
            q4 = q_ref[qrows, lo:lo + QUAD] * SCALE
            qd = jnp.concatenate([q4 * masks64[hl] for hl in range(4)], axis=0)
            k4 = kv_ref[pl.ds(koff, win), lo:lo + QUAD]
            return lax.dot_general(k4, qd, _NT, preferred_element_type=F32)

        def finish(st):
            st = st + bias_ref[hq, pl.ds(boff, win), :]
            m = jnp.max(st, axis=0, keepdims=True)
            e = jnp.exp(st - m)
            l = jnp.sum(e, axis=0, keepdims=True)
            p = (e * (1.0 / l)).astype(BF16)
            v4 = kv_ref[pl.ds(koff, win), QW + lo:QW + lo + QUAD]
            o2 = lax.dot_general(p, v4, _TN, preferred_element_type=F32)
            y4 = o2[:GRID_W]
            for hl in range(1, 4):
                y4 = jnp.where(slot == hl, o2[hl * GRID_W:(hl + 1) * GRID_W], y4)
            y_ref[qrows, lo:lo + QUAD] = y4.astype(BF16)

        return start, finish

    def store_mem_pair(blk, t, z):
        y_ref[blk, QW + t * LANES:QW + (t + 1) * LANES] = z.T.astype(BF16)

    units = [quad_unit(j, hq) for j in range(ROWS_PER_STEP_B) for hq in range(N_QUADS)]
    for s in range(ROWS_PER_STEP_B * GRID_W // LANES):
        blk = slice(s * LANES, (s + 1) * LANES)
        units += _mem_units(qm_ref, blk, mkv_ref, mvt_ref, masks128,
                            functools.partial(store_mem_pair, blk))
    _run_pipelined(units)


def _attn_b(proj, mkv, bias):
    seq, _ = proj.shape
    rows = seq // GRID_W
    tq = ROWS_PER_STEP_B * GRID_W
    slab = 2 * ROWS_PER_STEP_B * GRID_W
    return pl.pallas_call(
        functools.partial(_attn_b_kernel, rows=rows),
        grid=(rows // ROWS_PER_STEP_B,),
        in_specs=[
            pl.BlockSpec((tq, QW), lambda b: (b, 0)),
            pl.BlockSpec((tq, MW), lambda b: (b, 3 * QW // MW)),
            pl.BlockSpec((pl.Element(slab), pl.Element(2 * QW)),
                         lambda b: (_slab_start_row(b, rows) * GRID_W, QW)),
            _resident(bias.shape),
            _resident(mkv.shape),
        ],
        out_specs=pl.BlockSpec((tq, QW + MW), lambda b: (b, 0)),
        out_shape=jax.ShapeDtypeStruct((seq, QW + MW), BF16),
        scratch_shapes=[pltpu.VMEM((MW, mkv.shape[0]), BF16)],
        compiler_params=pltpu.CompilerParams(
            dimension_semantics=("arbitrary",), vmem_limit_bytes=VMEM_LIMIT_BYTES),
        name="attn_b",
    )(proj, proj, proj, bias, mkv)


def _post_kernel(x_ref, y_ref, wo_ref, gf_ref, wgu_ref, wd_ref, gn_ref, *rest, d_ff, last):
    x1 = x_ref[...] + jnp.dot(y_ref[...], wo_ref[...], preferred_element_type=F32)
    h = _rms(x1, gf_ref[...]).astype(BF16)
    gu = jnp.dot(h, wgu_ref[...], preferred_element_type=F32)
    g = gu[:, :d_ff]
    u = gu[:, d_ff:]
    a = (g * (1.0 / (1.0 + jnp.exp(-g))) * u).astype(BF16)
    x2 = x1 + jnp.dot(a, wd_ref[...], preferred_element_type=F32)
    if last:
        (o_ref,) = rest
        o_ref[...] = _rms(x2, gn_ref[...])
    else:
        wn_ref, o_ref, p_ref = rest
        o_ref[...] = x2
        hn = _rms(x2, gn_ref[...]).astype(BF16)
        p_ref[...] = jnp.dot(hn, wn_ref[...], preferred_element_type=F32).astype(BF16)


def _post(x, y, w_out, g_ffn, w_gu, w_d, g_next, w_next):
    s, d = x.shape
    d_ff = w_d.shape[0]
    last = w_next is None
    row = lambda i: (i, 0)
    in_specs = [pl.BlockSpec((ROW_TILE, d), row), pl.BlockSpec((ROW_TILE, y.shape[1]), row),
                _resident(w_out.shape), _resident((1, d)), _resident(w_gu.shape),
                _resident(w_d.shape), _resident((1, d))]
    args = [x, y, w_out, g_ffn.reshape(1, d), w_gu, w_d, g_next.reshape(1, d)]
    out_specs = [pl.BlockSpec((ROW_TILE, d), row)]
    out_shape = [jax.ShapeDtypeStruct((s, d), F32)]
    if not last:
        n = w_next.shape[1]
        in_specs.append(_resident(w_next.shape))
        args.append(w_next)
        out_specs.append(pl.BlockSpec((ROW_TILE, n), row))
        out_shape.append(jax.ShapeDtypeStruct((s, n), BF16))
    outs = pl.pallas_call(
        functools.partial(_post_kernel, d_ff=d_ff, last=last),
        grid=(s // ROW_TILE,),
        in_specs=in_specs,
        out_specs=out_specs,
        out_shape=out_shape,
        compiler_params=pltpu.CompilerParams(
            dimension_semantics=("arbitrary",), vmem_limit_bytes=VMEM_LIMIT_BYTES),
        name="post_last" if last else "post",
    )(*args)
    return (outs[0], None) if last else (outs[0], outs[1])


def kernel(x, mem, norm_mix, norm_ffn, norm_mem, norm_final, w_in_a, sink_a, w_in_b, rpb_b,
           w_mem_kv, w_out, w_gate_up, w_down):
    batch, seq, d = x.shape
    assert batch == 1 and mem.shape[0] == 1
    assert seq % (ROWS_PER_STEP_B * GRID_W) == 0 and seq % ROW_TILE == 0
    assert seq % (SUB_BLOCKS_A * BLOCK_A) == 0 and seq // BLOCK_A >= SUB_BLOCKS_A + 2
    depth = w_out.shape[0]
    xs = x.reshape(seq, d)
    cols_a = np.concatenate([_q_column_order_a(), np.arange(QW, w_in_a.shape[2])])
    w_in = [(w_in_a[i // 2][:, cols_a] if i % 2 == 0 else w_in_b[i // 2]).astype(BF16)
            for i in range(depth)]
    w_out_b = w_out.astype(BF16)
    w_gu_b = w_gate_up.astype(BF16)
    w_d_b = w_down.astype(BF16)

    mkv = _mem_kv(mem.reshape(mem.shape[1], d), norm_mem, w_mem_kv.astype(BF16))
    bias = _bias_tables(rpb_b)

    proj = _proj(xs, norm_mix[0], w_in[0])
    for i in range(depth):
        if i % 2 == 0:
            y = _attn_a(proj, mkv[i], sink_a[i // 2])
        else:
            y = _attn_b(proj, mkv[i], bias[i // 2])
        last = i == depth - 1
        xs, proj = _post(xs, y, w_out_b[i], norm_ffn[i], w_gu_b[i], w_d_b[i],
                         norm_final if last else norm_mix[i + 1],
                         None if last else w_in[i + 1])
    return xs.reshape(batch, seq, d)
```

```python
import functools

import numpy as np

import jax
import jax.numpy as jnp
from jax import lax
from jax.experimental import pallas as pl
from jax.experimental.pallas import tpu as pltpu

F32 = jnp.float32
BF16 = jnp.bfloat16

HEAD_DIM = 64
N_Q_HEADS = 12
N_KV_HEADS_A = 4
GROUP_A = N_Q_HEADS // N_KV_HEADS_A
N_MEM_HEADS = 4
WINDOW = 128
BLOCK_A = 128
SUB_BLOCKS_A = 4
GRID_W = 64
NA_ROWS = 8
NA_COLS = 16
ROWS_PER_STEP_B = 8
EPS = 1e-6
NEG_INF = -1e30
LOG2E = 1.4426950408889634
Q_SCALE = HEAD_DIM ** -0.5 * LOG2E

LANES = 128
QW = N_Q_HEADS * HEAD_DIM
KVW_A = N_KV_HEADS_A * HEAD_DIM
MW = N_MEM_HEADS * HEAD_DIM
QUAD = 4 * HEAD_DIM

VMEM_LIMIT_BYTES = 56 * 1024 * 1024
ROW_TILE = 512
SUB_TILE = 256

_NT = (((1,), (1,)), ((), ()))
_TN = (((0,), (0,)), ((), ()))


def _resident(shape):
    n = len(shape)
    return pl.BlockSpec(shape, lambda *_: (0,) * n)


def _layer_block(arr, layer):
    n = arr.ndim - 1
    return pl.BlockSpec((None,) + arr.shape[1:], lambda *_: (layer,) + (0,) * n)


def _rms(x, g):
    return x * lax.rsqrt(jnp.mean(x * x, axis=-1, keepdims=True) + EPS) * g


def _head_slot(shape):
    return lax.broadcasted_iota(jnp.int32, shape, 1) // HEAD_DIM


def _slot_masks(rows):
    slot = _head_slot((rows, QUAD))
    return [(slot == j).astype(BF16) for j in range(QUAD // HEAD_DIM)]


_STAGE_DELAY = (0, 3, 4)
STAGE_SLOTS = 4


def _run_pipelined(units):
    state = {}
    for t in range(len(units) + _STAGE_DELAY[-1]):
        for k, delay in enumerate(_STAGE_DELAY):
            u = t - delay
            if 0 <= u < len(units):
                state[u] = units[u][k](state.get(u))


def _pair_unit(k_all, q2, vt2, second_row, sinks, biases, store, stage):
    st_ref, e_ref, slot = stage

    def scores(_):
        k = k_all()
        st_ref[slot, :k.shape[0], :] = lax.dot_general(k, q2(), _NT, preferred_element_type=F32)
        return k.shape[0]

    def softmax(n_k):
        ms = []
        for p in range(2):
            cols = slice(p * LANES, (p + 1) * LANES)
            sth = st_ref[slot, :n_k, cols]
            if biases[p] is not None:
                sth = sth + biases[p]()
            m = jnp.max(sth, axis=0, keepdims=True)
            if sinks[p] is not None:
                m = jnp.maximum(m, sinks[p])
            e_ref[slot, :n_k, cols] = jnp.exp2(sth - m).astype(BF16)
            ms.append(m)
        return n_k, ms

    def weighted_values(state):
        n_k, ms = state
        v = vt2()
        n_v = v.shape[0]
        lhs = jnp.concatenate([v, jnp.ones((16, v.shape[1]), BF16)], axis=0)
        ot = jnp.dot(lhs, e_ref[slot, :n_k, :], preferred_element_type=F32)
        outs = []
        for p in range(2):
            l = ot[n_v:n_v + 1, p * LANES:(p + 1) * LANES]
            if sinks[p] is not None:
                l = l + jnp.exp2(sinks[p] - ms[p])
            r0 = p * second_row
            outs.append(ot[r0:r0 + HEAD_DIM, p * LANES:(p + 1) * LANES] * (1.0 / l))
        store(jnp.concatenate(outs, axis=0))

    return scores, softmax, weighted_values


def _mem_units(qm_ref, rows, mkv_ref, mvt_ref, masks, store_pair, stages):
    def q2(t):
        qm = qm_ref[rows, :]
        return jnp.concatenate([qm * masks[2 * t], qm * masks[2 * t + 1]], axis=0)

    return [_pair_unit(lambda: mkv_ref[:, :MW], functools.partial(q2, t),
                       lambda t=t: mvt_ref[t * LANES:(t + 1) * LANES, :], HEAD_DIM,
                       (None, None), (None, None), functools.partial(store_pair, t), stages[t])
            for t in range(N_MEM_HEADS // 2)]


def _mem_kv_kernel(mem_ref, g_ref, w_ref, o_ref):
    mem_n = _rms(mem_ref[...], g_ref[...])
    o_ref[...] = jnp.dot(mem_n.astype(BF16), w_ref[...],
                         preferred_element_type=F32).astype(BF16)


def _mem_kv(mem, norm_mem, w_mem_kv):
    depth, d, w = w_mem_kv.shape
    n_mem = mem.shape[0]
    return pl.pallas_call(
        _mem_kv_kernel,
        grid=(depth,),
        in_specs=[_resident((n_mem, d)), _resident((1, d)),
                  pl.BlockSpec((None, d, w), lambda i: (i, 0, 0))],
        out_specs=pl.BlockSpec((None, n_mem, w), lambda i: (i, 0, 0)),
        out_shape=jax.ShapeDtypeStruct((depth, n_mem, w), BF16),
        name="mem_kv",
    )(mem, norm_mem.reshape(1, d), w_mem_kv)


def _query_col_scale(width, n_q, n_qm):
    cs = np.ones((1, width), np.float32)
    cs[:, :n_q] = Q_SCALE
    cs[:, width - n_qm:] = Q_SCALE
    return jnp.asarray(cs)


def _proj_kernel(x_ref, g_ref, w_ref, cs_ref, o_ref):
    h = _rms(x_ref[...], g_ref[...]).astype(BF16)
    o_ref[...] = (jnp.dot(h, w_ref[...], preferred_element_type=F32) * cs_ref[...]).astype(BF16)


def _proj(x, g, layer, w, w_layer):
    s, d = x.shape
    n = w.shape[2]
    return pl.pallas_call(
        _proj_kernel,
        grid=(s // ROW_TILE,),
        in_specs=[pl.BlockSpec((ROW_TILE, d), lambda i: (i, 0)),
                  _layer_block(g, layer), _layer_block(w, w_layer), _resident((1, n))],
        out_specs=pl.BlockSpec((ROW_TILE, n), lambda i: (i, 0)),
        out_shape=jax.ShapeDtypeStruct((s, n), BF16),
        compiler_params=pltpu.CompilerParams(
            dimension_semantics=("arbitrary",), vmem_limit_bytes=VMEM_LIMIT_BYTES),
        name="in_proj",
    )(x, g, w, _query_col_scale(n, QW, MW))


def _alibi_slope(h):
    return 2.0 ** (-8.0 * (h + 1) / N_Q_HEADS)


def _q_column_order_a():
    cols = np.empty(QW, np.int32)
    for j in range(N_KV_HEADS_A):
        for g in range(GROUP_A):
            dst = QUAD * g + HEAD_DIM * j
            src = HEAD_DIM * (GROUP_A * j + g)
            cols[dst:dst + HEAD_DIM] = np.arange(src, src + HEAD_DIM)
    return cols


def _slab_block_a(i, nb):
    return jnp.clip(i * SUB_BLOCKS_A - 1, 0, nb - (SUB_BLOCKS_A + 2))


def _attn_a_kernel(sink_ref, q_ref, qm_ref, kv_ref, mkv_ref, y_ref, bias_ref, mvt_ref, vbt_ref,
                   st_ref, e_ref, *, nb):
    i = pl.program_id(0)
    band = 3 * BLOCK_A

    @pl.when(i == 0)
    def _():
        c = lax.broadcasted_iota(jnp.int32, (band, BLOCK_A), 0)
        r = lax.broadcasted_iota(jnp.int32, (band, BLOCK_A), 1)
        for var in range(3):
            dist = jnp.abs(r - c + var * BLOCK_A)
            allowed = dist <= WINDOW
            distf = dist.astype(F32)
            for h in range(N_Q_HEADS):
                bias_ref[var, h] = jnp.where(allowed, -(_alibi_slope(h) * LOG2E) * distf, NEG_INF)

        mvt_ref[...] = mkv_ref[:, MW:].T

    slab_blk = _slab_block_a(i, nb)
    masks = _slot_masks(BLOCK_A)

    def store_pair(rows, t, z):
        y_ref[rows, t * LANES:(t + 1) * LANES] = z.T.astype(BF16)

    def token_unit(s, t, off, var):
        rows = slice(s * BLOCK_A, (s + 1) * BLOCK_A)
        heads = (2 * t, 2 * t + 1)
        kvh = [h // GROUP_A for h in heads]

        def k_all():
            if t == 0:
                vbt_ref[s] = kv_ref[pl.ds(off, band), KVW_A:].T
            return kv_ref[pl.ds(off, band), :KVW_A]

        def q2():
            return jnp.concatenate(
                [q_ref[rows, (h % GROUP_A) * QUAD:(h % GROUP_A + 1) * QUAD] * masks[j]
                 for h, j in zip(heads, kvh)], axis=0)

        return _pair_unit(
            k_all, q2, lambda: vbt_ref[s, kvh[0] * HEAD_DIM:(kvh[1] + 1) * HEAD_DIM, :],
            (kvh[1] - kvh[0]) * HEAD_DIM, [sink_ref[h] * LOG2E for h in heads],
            [lambda h=h: bias_ref[var, h] for h in heads],
            functools.partial(store_pair, rows, t), (st_ref, e_ref, t % STAGE_SLOTS))

    units = []
    for s in range(SUB_BLOCKS_A):
        g_blk = i * SUB_BLOCKS_A + s
        band_blk = jnp.clip(g_blk - 1, 0, nb - 3)
        off = pl.multiple_of((band_blk - slab_blk) * BLOCK_A, BLOCK_A)
        var = g_blk - band_blk
        rows = slice(s * BLOCK_A, (s + 1) * BLOCK_A)
        units += [token_unit(s, t, off, var) for t in range(N_Q_HEADS // 2)]
        units += _mem_units(qm_ref, rows, mkv_ref, mvt_ref, masks,
                            lambda t, z, rows=rows: store_pair(rows, N_Q_HEADS // 2 + t, z),
                            [(st_ref, e_ref, (N_Q_HEADS // 2 + t) % STAGE_SLOTS)
                             for t in range(N_MEM_HEADS // 2)])
    _run_pipelined(units)


def _attn_a(proj, mkv, layer, sink):
    seq, _ = proj.shape
    nb = seq // BLOCK_A
    tq = SUB_BLOCKS_A * BLOCK_A
    slab = (SUB_BLOCKS_A + 2) * BLOCK_A
    return pl.pallas_call(
        functools.partial(_attn_a_kernel, nb=nb),
        grid=(nb // SUB_BLOCKS_A,),
        in_specs=[
            pl.BlockSpec(memory_space=pltpu.SMEM),
            pl.BlockSpec((tq, QW), lambda i: (i, 0)),
            pl.BlockSpec((tq, MW), lambda i: (i, (QW + 2 * KVW_A) // MW)),
            pl.BlockSpec((pl.Element(slab), pl.Element(2 * KVW_A)),
                         lambda i: (_slab_block_a(i, nb) * BLOCK_A, QW)),
            _layer_block(mkv, layer),
        ],
        out_specs=pl.BlockSpec((tq, QW + MW), lambda i: (i, 0)),
        out_shape=jax.ShapeDtypeStruct((seq, QW + MW), BF16),
        scratch_shapes=[pltpu.VMEM((3, N_Q_HEADS, 3 * BLOCK_A, BLOCK_A), F32),
                        pltpu.VMEM((MW, mkv.shape[1]), BF16),
                        pltpu.VMEM((SUB_BLOCKS_A, KVW_A, 3 * BLOCK_A), BF16),
                        pltpu.VMEM((STAGE_SLOTS, 3 * BLOCK_A, 2 * LANES), F32),
                        pltpu.VMEM((STAGE_SLOTS, 3 * BLOCK_A, 2 * LANES), BF16)],
        compiler_params=pltpu.CompilerParams(
            dimension_semantics=("arbitrary",), vmem_limit_bytes=VMEM_LIMIT_BYTES),
        name="attn_a",
    )(sink, proj, proj, proj, mkv)


N_DR = 2 * NA_ROWS - 1
N_DC = 2 * NA_COLS - 1
N_QUADS = N_Q_HEADS // 4


def _bias_table_kernel(rpb_ref, o_ref):
    li = pl.program_id(0)
    hq = pl.program_id(1)
    kc = lax.broadcasted_iota(jnp.int32, (GRID_W, QUAD), 0)
    lane = lax.broadcasted_iota(jnp.int32, (GRID_W, QUAD), 1)
    qc = lane % GRID_W
    col_start = jnp.clip(qc - NA_COLS // 2, 0, GRID_W - NA_COLS)
    in_win = (kc >= col_start) & (kc < col_start + NA_COLS)
    idx = jnp.clip(kc - qc, -(NA_COLS - 1), NA_COLS - 1) + NA_COLS - 1
    slot_row = _head_slot((1, QUAD))
    base = (li * N_Q_HEADS + 4 * hq) * (N_DR * N_DC)

    def body(d, carry):
        acc = jnp.zeros((GRID_W, QUAD), F32)
        for v in range(N_DC):
            a = [rpb_ref[base + hl * (N_DR * N_DC) + d * N_DC + v] for hl in range(4)]
            val = jnp.where(slot_row == 0, a[0],
                            jnp.where(slot_row == 1, a[1], jnp.where(slot_row == 2, a[2], a[3])))
            acc = jnp.where(idx == v, val, acc)
        o_ref[pl.ds(pl.multiple_of(d * GRID_W, GRID_W), GRID_W), :] = jnp.where(in_win, acc * LOG2E, NEG_INF)
        return carry

    lax.fori_loop(0, N_DR, body, 0)


def _bias_tables(rpb_b):
    n_layers = rpb_b.shape[0]
    return pl.pallas_call(
        _bias_table_kernel,
        grid=(n_layers, N_QUADS),
        in_specs=[pl.BlockSpec(memory_space=pltpu.SMEM)],
        out_specs=pl.BlockSpec((None, None, N_DR * GRID_W, QUAD),
                               lambda li, hq: (li, hq, 0, 0)),
        out_shape=jax.ShapeDtypeStruct((n_layers, N_QUADS, N_DR * GRID_W, QUAD), F32),
        name="rpb_table",
    )(rpb_b.reshape(-1))


def _slab_start_row(b, rows):
    return jnp.clip(b * ROWS_PER_STEP_B - NA_ROWS // 2, 0, rows - 2 * ROWS_PER_STEP_B)


def _attn_b_kernel(q_ref, qm_ref, kv_ref, bias_ref, mkv_ref, y_ref, mvt_ref, st_ref, e_ref,
                   *, rows):
    b = pl.program_id(0)
    win = NA_ROWS * GRID_W
    slot = _head_slot((GRID_W, QUAD))
    slab0 = _slab_start_row(b, rows)

    @pl.when(b == 0)
    def _():
        mvt_ref[...] = mkv_ref[:, MW:].T

    masks64 = _slot_masks(GRID_W)
    masks128 = _slot_masks(LANES)

    def quad_unit(j, hq):
        r = b * ROWS_PER_STEP_B + j
        row_start = jnp.clip(r - NA_ROWS // 2, 0, rows - NA_ROWS)
        koff = pl.multiple_of((row_start - slab0) * GRID_W, GRID_W)
        boff = pl.multiple_of((row_start - r + NA_ROWS - 1) * GRID_W, GRID_W)
        qrows = slice(j * GRID_W, (j + 1) * GRID_W)
        lo = hq * QUAD

        slot_st = (j * N_QUADS + hq) % STAGE_SLOTS

        def scores(_):
            q4 = q_ref[qrows, lo:lo + QUAD]
            qd = jnp.concatenate([q4 * masks64[hl] for hl in range(4)], axis=0)
            k4 = kv_ref[pl.ds(koff, win), lo:lo + QUAD]
            st_ref[slot_st] = lax.dot_general(k4, qd, _NT, preferred_element_type=F32)

        def softmax(_):
            st = st_ref[slot_st] + bias_ref[hq, pl.ds(boff, win), :]
            m = jnp.max(st, axis=0, keepdims=True)
            e = jnp.exp2(st - m)
            l = jnp.sum(e, axis=0, keepdims=True)
            e_ref[slot_st] = (e * (1.0 / l)).astype(BF16)

        def weighted_values(_):
            v4 = kv_ref[pl.ds(koff, win), QW + lo:QW + lo + QUAD]
            o2 = lax.dot_general(e_ref[slot_st], v4, _TN, preferred_element_type=F32)
            y4 = o2[:GRID_W]
            for hl in range(1, 4):
                y4 = jnp.where(slot == hl, o2[hl * GRID_W:(hl + 1) * GRID_W], y4)
            y_ref[qrows, lo:lo + QUAD] = y4.astype(BF16)

        return scores, softmax, weighted_values

    def store_mem_pair(blk, t, z):
        y_ref[blk, QW + t * LANES:QW + (t + 1) * LANES] = z.T.astype(BF16)

    units = [quad_unit(j, hq) for j in range(ROWS_PER_STEP_B) for hq in range(N_QUADS)]
    for s in range(ROWS_PER_STEP_B * GRID_W // LANES):
        blk = slice(s * LANES, (s + 1) * LANES)
        units += _mem_units(qm_ref, blk, mkv_ref, mvt_ref, masks128,
                            functools.partial(store_mem_pair, blk),
                            [(st_ref, e_ref, (s * (N_MEM_HEADS // 2) + t) % STAGE_SLOTS)
                             for t in range(N_MEM_HEADS // 2)])
    _run_pipelined(units)


def _attn_b(proj, mkv, layer, bias, bias_layer):
    seq, _ = proj.shape
    rows = seq // GRID_W
    tq = ROWS_PER_STEP_B * GRID_W
    slab = 2 * ROWS_PER_STEP_B * GRID_W
    return pl.pallas_call(
        functools.partial(_attn_b_kernel, rows=rows),
        grid=(rows // ROWS_PER_STEP_B,),
        in_specs=[
            pl.BlockSpec((tq, QW), lambda b: (b, 0)),
            pl.BlockSpec((tq, MW), lambda b: (b, 3 * QW // MW)),
            pl.BlockSpec((pl.Element(slab), pl.Element(2 * QW)),
                         lambda b: (_slab_start_row(b, rows) * GRID_W, QW)),
            _layer_block(bias, bias_layer),
            _layer_block(mkv, layer),
        ],
        out_specs=pl.BlockSpec((tq, QW + MW), lambda b: (b, 0)),
        out_shape=jax.ShapeDtypeStruct((seq, QW + MW), BF16),
        scratch_shapes=[pltpu.VMEM((MW, mkv.shape[1]), BF16),
                        pltpu.VMEM((STAGE_SLOTS, NA_ROWS * GRID_W, QUAD), F32),
                        pltpu.VMEM((STAGE_SLOTS, NA_ROWS * GRID_W, QUAD), BF16)],
        compiler_params=pltpu.CompilerParams(
            dimension_semantics=("arbitrary",), vmem_limit_bytes=VMEM_LIMIT_BYTES),
        name="attn_b",
    )(proj, proj, proj, bias, mkv)


def _post_kernel(x_ref, y_ref, wo_ref, gf_ref, wgu_ref, wd_ref, gn_ref, *rest, d_ff, last):
    for half in range(ROW_TILE // SUB_TILE):
        rows = slice(half * SUB_TILE, (half + 1) * SUB_TILE)
        x1 = x_ref[rows, :] + jnp.dot(y_ref[rows, :], wo_ref[...], preferred_element_type=F32)
        h = _rms(x1, gf_ref[...]).astype(BF16)
        gu = jnp.dot(h, wgu_ref[...], preferred_element_type=F32)
        g = gu[:, :d_ff]
        u = gu[:, d_ff:]
        a = (g * (1.0 / (1.0 + jnp.exp(-g))) * u).astype(BF16)
        x2 = x1 + jnp.dot(a, wd_ref[...], preferred_element_type=F32)
        if last:
            (o_ref,) = rest
            o_ref[rows, :] = _rms(x2, gn_ref[...])
        else:
            wn_ref, cs_ref, o_ref, p_ref = rest
            o_ref[rows, :] = x2
            hn = _rms(x2, gn_ref[...]).astype(BF16)
            p_ref[rows, :] = (jnp.dot(hn, wn_ref[...], preferred_element_type=F32)
                              * cs_ref[...]).astype(BF16)


def _post(x, y, layer, w_out, g_ffn, w_gu, w_d, g_next, next_layer, w_next, w_next_layer):
    s, d = x.shape
    d_ff = w_d.shape[1]
    last = w_next is None
    row = lambda i: (i, 0)
    in_specs = [pl.BlockSpec((ROW_TILE, d), row), pl.BlockSpec((ROW_TILE, y.shape[1]), row),
                _layer_block(w_out, layer), _layer_block(g_ffn, layer), _layer_block(w_gu, layer),
                _layer_block(w_d, layer), _layer_block(g_next, next_layer)]
    args = [x, y, w_out, g_ffn, w_gu, w_d, g_next]
    out_specs = [pl.BlockSpec((ROW_TILE, d), row)]
    out_shape = [jax.ShapeDtypeStruct((s, d), F32)]
    if not last:
        n = w_next.shape[2]
        in_specs += [_layer_block(w_next, w_next_layer), _resident((1, n))]
        args += [w_next, _query_col_scale(n, QW, MW)]
        out_specs.append(pl.BlockSpec((ROW_TILE, n), row))
        out_shape.append(jax.ShapeDtypeStruct((s, n), BF16))
    outs = pl.pallas_call(
        functools.partial(_post_kernel, d_ff=d_ff, last=last),
        grid=(s // ROW_TILE,),
        in_specs=in_specs,
        out_specs=out_specs,
        out_shape=out_shape,
        compiler_params=pltpu.CompilerParams(
            dimension_semantics=("arbitrary",), vmem_limit_bytes=VMEM_LIMIT_BYTES),
        name="post_last" if last else "post",
    )(*args)
    return (outs[0], None) if last else (outs[0], outs[1])


def kernel(x, mem, norm_mix, norm_ffn, norm_mem, norm_final, w_in_a, sink_a, w_in_b, rpb_b,
           w_mem_kv, w_out, w_gate_up, w_down):
    batch, seq, d = x.shape
    assert batch == 1 and mem.shape[0] == 1
    assert seq % (ROWS_PER_STEP_B * GRID_W) == 0 and seq % ROW_TILE == 0
    assert seq % (SUB_BLOCKS_A * BLOCK_A) == 0 and seq // BLOCK_A >= SUB_BLOCKS_A + 2
    depth = w_out.shape[0]
    xs = x.reshape(seq, d)
    cols_a = np.concatenate([_q_column_order_a(), np.arange(QW, w_in_a.shape[2])])
    w_in = (w_in_a[:, :, cols_a].astype(BF16), w_in_b.astype(BF16))
    w_out_b = w_out.astype(BF16)
    w_gu_b = w_gate_up.astype(BF16)
    w_d_b = w_down.astype(BF16)
    g_mix = norm_mix.reshape(depth, 1, d)
    g_ffn = norm_ffn.reshape(depth, 1, d)
    g_final = norm_final.reshape(1, 1, d)

    mkv = _mem_kv(mem.reshape(mem.shape[1], d), norm_mem, w_mem_kv.astype(BF16))
    bias = _bias_tables(rpb_b)

    proj = _proj(xs, g_mix, 0, w_in[0], 0)
    for i in range(depth):
        if i % 2 == 0:
            y = _attn_a(proj, mkv, i, sink_a[i // 2])
        else:
            y = _attn_b(proj, mkv, i, bias, i // 2)
        if i == depth - 1:
            xs, proj = _post(xs, y, i, w_out_b, g_ffn, w_gu_b, w_d_b, g_final, 0, None, 0)
        else:
            xs, proj = _post(xs, y, i, w_out_b, g_ffn, w_gu_b, w_d_b, g_mix, i + 1,
                             w_in[(i + 1) % 2], (i + 1) // 2)
    return xs.reshape(batch, seq, d)
```

```python
import functools

import numpy as np

import jax
import jax.numpy as jnp
from jax import lax
from jax.experimental import pallas as pl
from jax.experimental.pallas import tpu as pltpu

F32 = jnp.float32
BF16 = jnp.bfloat16

HEAD_DIM = 64
N_Q_HEADS = 12
N_KV_HEADS_A = 4
GROUP_A = N_Q_HEADS // N_KV_HEADS_A
N_MEM_HEADS = 4
WINDOW = 128
BLOCK_A = 128
SUB_BLOCKS_A = 4
GRID_W = 64
NA_ROWS = 8
NA_COLS = 16
ROWS_PER_STEP_B = 8
EPS = 1e-6
NEG_INF = -1e30
LOG2E = 1.4426950408889634
Q_SCALE = HEAD_DIM ** -0.5 * LOG2E

LANES = 128
QW = N_Q_HEADS * HEAD_DIM
KVW_A = N_KV_HEADS_A * HEAD_DIM
MW = N_MEM_HEADS * HEAD_DIM
QUAD = 4 * HEAD_DIM

VMEM_LIMIT_BYTES = 56 * 1024 * 1024
ROW_TILE = 512
SUB_TILE = 256

_NT = (((1,), (1,)), ((), ()))
_TN = (((0,), (0,)), ((), ()))


def _resident(shape):
    n = len(shape)
    return pl.BlockSpec(shape, lambda *_: (0,) * n)


def _layer_block(arr, layer):
    n = arr.ndim - 1
    return pl.BlockSpec((None,) + arr.shape[1:], lambda *_: (layer,) + (0,) * n)


def _rms(x, g):
    return x * lax.rsqrt(jnp.mean(x * x, axis=-1, keepdims=True) + EPS) * g


def _head_slot(shape):
    return lax.broadcasted_iota(jnp.int32, shape, 1) // HEAD_DIM


def _slot_masks(rows):
    slot = _head_slot((rows, QUAD))
    return [(slot == j).astype(BF16) for j in range(QUAD // HEAD_DIM)]


STAGE_DELAY_A = (0, 3, 3)
STAGE_DELAY_B = (0, 1, 1)


def _run_pipelined(units, delays):
    state = {}
    for t in range(len(units) + delays[-1]):
        for k, delay in enumerate(delays):
            u = t - delay
            if 0 <= u < len(units):
                state[u] = units[u][k](state.get(u))


def _pair_unit(k_all, q2, vt2, second_row, sinks, biases, store):
    def scores(_):
        st = lax.dot_general(k_all(), q2(), _NT, preferred_element_type=F32)
        sts, ms = [], []
        for p in range(2):
            sth = st[:, p * LANES:(p + 1) * LANES]
            if biases[p] is not None:
                sth = sth + biases[p]()
            m = jnp.max(sth, axis=0, keepdims=True)
            if sinks[p] is not None:
                m = jnp.maximum(m, sinks[p])
            sts.append(sth)
            ms.append(m)
        return sts, ms

    def softmax(state):
        sts, ms = state
        es = [jnp.exp2(sth - m).astype(BF16) for sth, m in zip(sts, ms)]
        return jnp.concatenate(es, axis=1), ms

    def weighted_values(state):
        e2, ms = state
        v = vt2()
        n_v = v.shape[0]
        lhs = jnp.concatenate([v, jnp.ones((16, v.shape[1]), BF16)], axis=0)
        ot = jnp.dot(lhs, e2, preferred_element_type=F32)
        outs = []
        for p in range(2):
            l = ot[n_v:n_v + 1, p * LANES:(p + 1) * LANES]
            if sinks[p] is not None:
                l = l + jnp.exp2(sinks[p] - ms[p])
            r0 = p * second_row
            outs.append(ot[r0:r0 + HEAD_DIM, p * LANES:(p + 1) * LANES] * (1.0 / l))
        store(jnp.concatenate(outs, axis=0))

    return scores, softmax, weighted_values


def _mem_units(qm_ref, rows, mkv_ref, mvt_ref, masks, store_pair):
    def q2(t):
        qm = qm_ref[rows, :]
        return jnp.concatenate([qm * masks[2 * t], qm * masks[2 * t + 1]], axis=0)

    return [_pair_unit(lambda: mkv_ref[:, :MW], functools.partial(q2, t),
                       lambda t=t: mvt_ref[t * LANES:(t + 1) * LANES, :], HEAD_DIM,
                       (None, None), (None, None), functools.partial(store_pair, t))
            for t in range(N_MEM_HEADS // 2)]


def _mem_kv_kernel(mem_ref, g_ref, w_ref, o_ref):
    mem_n = _rms(mem_ref[...], g_ref[...])
    o_ref[...] = jnp.dot(mem_n.astype(BF16), w_ref[...],
                         preferred_element_type=F32).astype(BF16)


def _mem_kv(mem, norm_mem, w_mem_kv):
    depth, d, w = w_mem_kv.shape
    n_mem = mem.shape[0]
    return pl.pallas_call(
        _mem_kv_kernel,
        grid=(depth,),
        in_specs=[_resident((n_mem, d)), _resident((1, d)),
                  pl.BlockSpec((None, d, w), lambda i: (i, 0, 0))],
        out_specs=pl.BlockSpec((None, n_mem, w), lambda i: (i, 0, 0)),
        out_shape=jax.ShapeDtypeStruct((depth, n_mem, w), BF16),
        name="mem_kv",
    )(mem, norm_mem.reshape(1, d), w_mem_kv)


def _query_col_scale(width, n_q, n_qm):
    cs = np.ones((1, width), np.float32)
    cs[:, :n_q] = Q_SCALE
    cs[:, width - n_qm:] = Q_SCALE
    return jnp.asarray(cs)


def _proj_kernel(x_ref, g_ref, w_ref, cs_ref, o_ref):
    h = _rms(x_ref[...], g_ref[...]).astype(BF16)
    o_ref[...] = (jnp.dot(h, w_ref[...], preferred_element_type=F32) * cs_ref[...]).astype(BF16)


def _proj(x, g, layer, w, w_layer):
    s, d = x.shape
    n = w.shape[2]
    return pl.pallas_call(
        _proj_kernel,
        grid=(s // ROW_TILE,),
        in_specs=[pl.BlockSpec((ROW_TILE, d), lambda i: (i, 0)),
                  _layer_block(g, layer), _layer_block(w, w_layer), _resident((1, n))],
        out_specs=pl.BlockSpec((ROW_TILE, n), lambda i: (i, 0)),
        out_shape=jax.ShapeDtypeStruct((s, n), BF16),
        compiler_params=pltpu.CompilerParams(
            dimension_semantics=("arbitrary",), vmem_limit_bytes=VMEM_LIMIT_BYTES),
        name="in_proj",
    )(x, g, w, _query_col_scale(n, QW, MW))


def _alibi_slope(h):
    return 2.0 ** (-8.0 * (h + 1) / N_Q_HEADS)


def _regroup_q_columns_a(w):
    heads = [GROUP_A * j + g for g in range(GROUP_A) for j in range(N_KV_HEADS_A)]
    return jnp.concatenate([w[:, :, h * HEAD_DIM:(h + 1) * HEAD_DIM] for h in heads]
                           + [w[:, :, QW:]], axis=2)


def _slab_block_a(i, nb):
    return jnp.clip(i * SUB_BLOCKS_A - 1, 0, nb - (SUB_BLOCKS_A + 2))


def _attn_a_kernel(sink_ref, q_ref, qm_ref, kv_ref, mkv_ref, y_ref, bias_ref, mvt_ref, vbt_ref,
                   *, nb):
    i = pl.program_id(0)
    band = 3 * BLOCK_A

    @pl.when(i == 0)
    def _():
        c = lax.broadcasted_iota(jnp.int32, (band, BLOCK_A), 0)
        r = lax.broadcasted_iota(jnp.int32, (band, BLOCK_A), 1)
        for var in range(3):
            dist = jnp.abs(r - c + var * BLOCK_A)
            allowed = dist <= WINDOW
            distf = dist.astype(F32)
            for h in range(N_Q_HEADS):
                bias_ref[var, h] = jnp.where(allowed, -(_alibi_slope(h) * LOG2E) * distf, NEG_INF)

        mvt_ref[...] = mkv_ref[:, MW:].T

    slab_blk = _slab_block_a(i, nb)
    masks = _slot_masks(BLOCK_A)

    def store_pair(rows, t, z):
        y_ref[rows, t * LANES:(t + 1) * LANES] = z.T.astype(BF16)

    def token_unit(s, t, off, var):
        rows = slice(s * BLOCK_A, (s + 1) * BLOCK_A)
        heads = (2 * t, 2 * t + 1)
        kvh = [h // GROUP_A for h in heads]

        def k_all():
            if t == 0:
                vbt_ref[s] = kv_ref[pl.ds(off, band), KVW_A:].T
            return kv_ref[pl.ds(off, band), :KVW_A]

        def q2():
            return jnp.concatenate(
                [q_ref[rows, (h % GROUP_A) * QUAD:(h % GROUP_A + 1) * QUAD] * masks[j]
                 for h, j in zip(heads, kvh)], axis=0)

        return _pair_unit(
            k_all, q2, lambda: vbt_ref[s, kvh[0] * HEAD_DIM:(kvh[1] + 1) * HEAD_DIM, :],
            (kvh[1] - kvh[0]) * HEAD_DIM, [sink_ref[h] * LOG2E for h in heads],
            [lambda h=h: bias_ref[var, h] for h in heads],
            functools.partial(store_pair, rows, t))

    units = []
    for s in range(SUB_BLOCKS_A):
        g_blk = i * SUB_BLOCKS_A + s
        band_blk = jnp.clip(g_blk - 1, 0, nb - 3)
        off = pl.multiple_of((band_blk - slab_blk) * BLOCK_A, BLOCK_A)
        var = g_blk - band_blk
        rows = slice(s * BLOCK_A, (s + 1) * BLOCK_A)
        units += [token_unit(s, t, off, var) for t in range(N_Q_HEADS // 2)]
        units += _mem_units(qm_ref, rows, mkv_ref, mvt_ref, masks,
                            lambda t, z, rows=rows: store_pair(rows, N_Q_HEADS // 2 + t, z))
    _run_pipelined(units, STAGE_DELAY_A)


def _attn_a(proj, mkv, layer, sink):
    seq, _ = proj.shape
    nb = seq // BLOCK_A
    tq = SUB_BLOCKS_A * BLOCK_A
    slab = (SUB_BLOCKS_A + 2) * BLOCK_A
    return pl.pallas_call(
        functools.partial(_attn_a_kernel, nb=nb),
        grid=(nb // SUB_BLOCKS_A,),
        in_specs=[
            pl.BlockSpec(memory_space=pltpu.SMEM),
            pl.BlockSpec((tq, QW), lambda i: (i, 0)),
            pl.BlockSpec((tq, MW), lambda i: (i, (QW + 2 * KVW_A) // MW)),
            pl.BlockSpec((pl.Element(slab), pl.Element(2 * KVW_A)),
                         lambda i: (_slab_block_a(i, nb) * BLOCK_A, QW)),
            _layer_block(mkv, layer),
        ],
        out_specs=pl.BlockSpec((tq, QW + MW), lambda i: (i, 0)),
        out_shape=jax.ShapeDtypeStruct((seq, QW + MW), BF16),
        scratch_shapes=[pltpu.VMEM((3, N_Q_HEADS, 3 * BLOCK_A, BLOCK_A), F32),
                        pltpu.VMEM((MW, mkv.shape[1]), BF16),
                        pltpu.VMEM((SUB_BLOCKS_A, KVW_A, 3 * BLOCK_A), BF16)],
        compiler_params=pltpu.CompilerParams(
            dimension_semantics=("arbitrary",), vmem_limit_bytes=VMEM_LIMIT_BYTES),
        name="attn_a",
    )(sink, proj, proj, proj, mkv)


N_DR = 2 * NA_ROWS - 1
N_DC = 2 * NA_COLS - 1
N_QUADS = N_Q_HEADS // 4


def _bias_table_kernel(rpb_ref, o_ref):
    li = pl.program_id(0)
    hq = pl.program_id(1)
    kc = lax.broadcasted_iota(jnp.int32, (GRID_W, QUAD), 0)
    lane = lax.broadcasted_iota(jnp.int32, (GRID_W, QUAD), 1)
    qc = lane % GRID_W
    col_start = jnp.clip(qc - NA_COLS // 2, 0, GRID_W - NA_COLS)
    in_win = (kc >= col_start) & (kc < col_start + NA_COLS)
    idx = jnp.clip(kc - qc, -(NA_COLS - 1), NA_COLS - 1) + NA_COLS - 1
    slot_row = _head_slot((1, QUAD))
    base = (li * N_Q_HEADS + 4 * hq) * (N_DR * N_DC)

    def body(d, carry):
        acc = jnp.zeros((GRID_W, QUAD), F32)
        for v in range(N_DC):
            a = [rpb_ref[base + hl * (N_DR * N_DC) + d * N_DC + v] for hl in range(4)]
            val = jnp.where(slot_row == 0, a[0],
                            jnp.where(slot_row == 1, a[1], jnp.where(slot_row == 2, a[2], a[3])))
            acc = jnp.where(idx == v, val, acc)
        o_ref[pl.ds(pl.multiple_of(d * GRID_W, GRID_W), GRID_W), :] = jnp.where(in_win, acc * LOG2E, NEG_INF)
        return carry

    lax.fori_loop(0, N_DR, body, 0)


def _bias_tables(rpb_b):
    n_layers = rpb_b.shape[0]
    return pl.pallas_call(
        _bias_table_kernel,
        grid=(n_layers, N_QUADS),
        in_specs=[pl.BlockSpec(memory_space=pltpu.SMEM)],
        out_specs=pl.BlockSpec((None, None, N_DR * GRID_W, QUAD),
                               lambda li, hq: (li, hq, 0, 0)),
        out_shape=jax.ShapeDtypeStruct((n_layers, N_QUADS, N_DR * GRID_W, QUAD), F32),
        name="rpb_table",
    )(rpb_b.reshape(-1))


def _slab_start_row(b, rows):
    return jnp.clip(b * ROWS_PER_STEP_B - NA_ROWS // 2, 0, rows - 2 * ROWS_PER_STEP_B)


def _attn_b_kernel(q_ref, qm_ref, kv_ref, bias_ref, mkv_ref, y_ref, mvt_ref, *, rows):
    b = pl.program_id(0)
    win = NA_ROWS * GRID_W
    slot = _head_slot((GRID_W, QUAD))
    slab0 = _slab_start_row(b, rows)

    @pl.when(b == 0)
    def _():
        mvt_ref[...] = mkv_ref[:, MW:].T

    masks64 = _slot_masks(GRID_W)
    masks128 = _slot_masks(LANES)

    def quad_unit(j, hq):
        r = b * ROWS_PER_STEP_B + j
        row_start = jnp.clip(r - NA_ROWS // 2, 0, rows - NA_ROWS)
        koff = pl.multiple_of((row_start - slab0) * GRID_W, GRID_W)
        boff = pl.multiple_of((row_start - r + NA_ROWS - 1) * GRID_W, GRID_W)
        qrows = slice(j * GRID_W, (j + 1) * GRID_W)
        lo = hq * QUAD

        def scores(_):
            q4 = q_ref[qrows, lo:lo + QUAD]
            qd = jnp.concatenate([q4 * masks64[hl] for hl in range(4)], axis=0)
            k4 = kv_ref[pl.ds(koff, win), lo:lo + QUAD]
            st = lax.dot_general(k4, qd, _NT, preferred_element_type=F32)
            st = st + bias_ref[hq, pl.ds(boff, win), :]
            return st, jnp.max(st, axis=0, keepdims=True)

        def softmax(state):
            st, m = state
            e = jnp.exp2(st - m)
            return e.astype(BF16), jnp.sum(e, axis=0, keepdims=True)

        def weighted_values(state):
            p, l = state
            v4 = kv_ref[pl.ds(koff, win), QW + lo:QW + lo + QUAD]
            o2 = lax.dot_general(p, v4, _TN, preferred_element_type=F32)
            inv = jnp.broadcast_to(1.0 / l, (LANES, QUAD)).T
            y4 = o2[:GRID_W] * jnp.tile(inv[:GRID_W], (1, 2))
            for hl in range(1, 4):
                part = o2[hl * GRID_W:(hl + 1) * GRID_W] * jnp.tile(inv[hl * GRID_W:(hl + 1) * GRID_W], (1, 2))
                y4 = jnp.where(slot == hl, part, y4)
            y_ref[qrows, lo:lo + QUAD] = y4.astype(BF16)

        return scores, softmax, weighted_values

    def store_mem_pair(blk, t, z):
        y_ref[blk, QW + t * LANES:QW + (t + 1) * LANES] = z.T.astype(BF16)

    units = [quad_unit(j, hq) for j in range(ROWS_PER_STEP_B) for hq in range(N_QUADS)]
    for s in range(ROWS_PER_STEP_B * GRID_W // LANES):
        blk = slice(s * LANES, (s + 1) * LANES)
        units += _mem_units(qm_ref, blk, mkv_ref, mvt_ref, masks128,
                            functools.partial(store_mem_pair, blk))
    _run_pipelined(units, STAGE_DELAY_B)


def _attn_b(proj, mkv, layer, bias, bias_layer):
    seq, _ = proj.shape
    rows = seq // GRID_W
    tq = ROWS_PER_STEP_B * GRID_W
    slab = 2 * ROWS_PER_STEP_B * GRID_W
    return pl.pallas_call(
        functools.partial(_attn_b_kernel, rows=rows),
        grid=(rows // ROWS_PER_STEP_B,),
        in_specs=[
            pl.BlockSpec((tq, QW), lambda b: (b, 0)),
            pl.BlockSpec((tq, MW), lambda b: (b, 3 * QW // MW)),
            pl.BlockSpec((pl.Element(slab), pl.Element(2 * QW)),
                         lambda b: (_slab_start_row(b, rows) * GRID_W, QW)),
            _layer_block(bias, bias_layer),
            _layer_block(mkv, layer),
        ],
        out_specs=pl.BlockSpec((tq, QW + MW), lambda b: (b, 0)),
        out_shape=jax.ShapeDtypeStruct((seq, QW + MW), BF16),
        scratch_shapes=[pltpu.VMEM((MW, mkv.shape[1]), BF16)],
        compiler_params=pltpu.CompilerParams(
            dimension_semantics=("arbitrary",), vmem_limit_bytes=VMEM_LIMIT_BYTES),
        name="attn_b",
    )(proj, proj, proj, bias, mkv)


def _post_kernel(x_ref, y_ref, wo_ref, gf_ref, wgu_ref, wd_ref, gn_ref, *rest, d_ff, last):
    for half in range(ROW_TILE // SUB_TILE):
        rows = slice(half * SUB_TILE, (half + 1) * SUB_TILE)
        x1 = x_ref[rows, :] + jnp.dot(y_ref[rows, :], wo_ref[...], preferred_element_type=F32)
        h = _rms(x1, gf_ref[...]).astype(BF16)
        gu = jnp.dot(h, wgu_ref[...], preferred_element_type=F32)
        g = gu[:, :d_ff]
        u = gu[:, d_ff:]
        a = (g * (1.0 / (1.0 + jnp.exp(-g))) * u).astype(BF16)
        x2 = x1 + jnp.dot(a, wd_ref[...], preferred_element_type=F32)
        if last:
            (o_ref,) = rest
            o_ref[rows, :] = _rms(x2, gn_ref[...])
        else:
            wn_ref, cs_ref, o_ref, p_ref = rest
            o_ref[rows, :] = x2
            hn = _rms(x2, gn_ref[...]).astype(BF16)
            p_ref[rows, :] = (jnp.dot(hn, wn_ref[...], preferred_element_type=F32)
                              * cs_ref[...]).astype(BF16)


def _post(x, y, layer, w_out, g_ffn, w_gu, w_d, g_next, next_layer, w_next, w_next_layer):
    s, d = x.shape
    d_ff = w_d.shape[1]
    last = w_next is None
    row = lambda i: (i, 0)
    in_specs = [pl.BlockSpec((ROW_TILE, d), row), pl.BlockSpec((ROW_TILE, y.shape[1]), row),
                _layer_block(w_out, layer), _layer_block(g_ffn, layer), _layer_block(w_gu, layer),
                _layer_block(w_d, layer), _layer_block(g_next, next_layer)]
    args = [x, y, w_out, g_ffn, w_gu, w_d, g_next]
    out_specs = [pl.BlockSpec((ROW_TILE, d), row)]
    out_shape = [jax.ShapeDtypeStruct((s, d), F32)]
    if not last:
        n = w_next.shape[2]
        in_specs += [_layer_block(w_next, w_next_layer), _resident((1, n))]
        args += [w_next, _query_col_scale(n, QW, MW)]
        out_specs.append(pl.BlockSpec((ROW_TILE, n), row))
        out_shape.append(jax.ShapeDtypeStruct((s, n), BF16))
    outs = pl.pallas_call(
        functools.partial(_post_kernel, d_ff=d_ff, last=last),
        grid=(s // ROW_TILE,),
        in_specs=in_specs,
        out_specs=out_specs,
        out_shape=out_shape,
        compiler_params=pltpu.CompilerParams(
            dimension_semantics=("arbitrary",), vmem_limit_bytes=VMEM_LIMIT_BYTES),
        name="post_last" if last else "post",
    )(*args)
    return (outs[0], None) if last else (outs[0], outs[1])


def kernel(x, mem, norm_mix, norm_ffn, norm_mem, norm_final, w_in_a, sink_a, w_in_b, rpb_b,
           w_mem_kv, w_out, w_gate_up, w_down):
    batch, seq, d = x.shape
    assert batch == 1 and mem.shape[0] == 1
    assert seq % (ROWS_PER_STEP_B * GRID_W) == 0 and seq % ROW_TILE == 0
    assert seq % (SUB_BLOCKS_A * BLOCK_A) == 0 and seq // BLOCK_A >= SUB_BLOCKS_A + 2
    depth = w_out.shape[0]
    xs = x.reshape(seq, d)
    w_in = (_regroup_q_columns_a(w_in_a.astype(BF16)), w_in_b.astype(BF16))
    w_out_b = w_out.astype(BF16)
    w_gu_b = w_gate_up.astype(BF16)
    w_d_b = w_down.astype(BF16)
    g_mix = norm_mix.reshape(depth, 1, d)
    g_ffn = norm_ffn.reshape(depth, 1, d)
    g_final = norm_final.reshape(1, 1, d)

    mkv = _mem_kv(mem.reshape(mem.shape[1], d), norm_mem, w_mem_kv.astype(BF16))
    bias = _bias_tables(rpb_b)

    proj = _proj(xs, g_mix, 0, w_in[0], 0)
    for i in range(depth):
        if i % 2 == 0:
            y = _attn_a(proj, mkv, i, sink_a[i // 2])
        else:
            y = _attn_b(proj, mkv, i, bias, i // 2)
        if i == depth - 1:
            xs, proj = _post(xs, y, i, w_out_b, g_ffn, w_gu_b, w_d_b, g_final, 0, None, 0)
        else:
            xs, proj = _post(xs, y, i, w_out_b, g_ffn, w_gu_b, w_d_b, g_mix, i + 1,
                             w_in[(i + 1) % 2], (i + 1) // 2)
    return xs.reshape(batch, seq, d)
```

```python
import functools

import numpy as np

import jax
import jax.numpy as jnp
from jax import lax
from jax.experimental import pallas as pl
from jax.experimental.pallas import tpu as pltpu

F32 = jnp.float32
BF16 = jnp.bfloat16

HEAD_DIM = 64
N_Q_HEADS = 12
N_KV_HEADS_A = 4
GROUP_A = N_Q_HEADS // N_KV_HEADS_A
N_MEM_HEADS = 4
WINDOW = 128
BLOCK_A = 128
SUB_BLOCKS_A = 4
GRID_W = 64
NA_ROWS = 8
NA_COLS = 16
ROWS_PER_STEP_B = 8
EPS = 1e-6
NEG_INF = -1e30
LOG2E = 1.4426950408889634
Q_SCALE = HEAD_DIM ** -0.5 * LOG2E

LANES = 128
QW = N_Q_HEADS * HEAD_DIM
KVW_A = N_KV_HEADS_A * HEAD_DIM
MW = N_MEM_HEADS * HEAD_DIM
QUAD = 4 * HEAD_DIM

VMEM_LIMIT_BYTES = 56 * 1024 * 1024
ROW_TILE = 512
SUB_TILE = 256

_NT = (((1,), (1,)), ((), ()))
_TN = (((0,), (0,)), ((), ()))


def _resident(shape):
    n = len(shape)
    return pl.BlockSpec(shape, lambda *_: (0,) * n)


def _layer_block(arr, layer):
    n = arr.ndim - 1
    return pl.BlockSpec((None,) + arr.shape[1:], lambda *_: (layer,) + (0,) * n)


W_CHUNK_BYTES = 1 << 20
BF16_ROWS = 16


def _chunk_rows(n_rows, n_cols):
    rc = n_rows
    while rc * n_cols * 4 > W_CHUNK_BYTES and rc % (2 * BF16_ROWS) == 0:
        rc //= 2
    return rc


def _weight_scratch(w):
    _, n_rows, n_cols = w.shape
    return [pltpu.VMEM((n_rows, n_cols), BF16),
            pltpu.VMEM((2, _chunk_rows(n_rows, n_cols), n_cols), F32),
            pltpu.SemaphoreType.DMA((2,))]


def _load_cast(w_hbm, layer, dst_ref, stage_ref, sem_ref, regroup=None):
    rc = stage_ref.shape[1]
    n_chunks = dst_ref.shape[0] // rc

    def copy(c, slot):
        return pltpu.make_async_copy(w_hbm.at[layer, pl.ds(c * rc, rc), :], stage_ref.at[slot],
                                     sem_ref.at[slot])

    copy(0, 0).start()

    def body(c, carry):
        slot = lax.rem(c, 2)

        @pl.when(c + 1 < n_chunks)
        def _():
            copy(c + 1, 1 - slot).start()

        copy(c, slot).wait()
        rows = pl.ds(pl.multiple_of(c * rc, rc), rc)
        blk = stage_ref[slot]
        if regroup is None:
            dst_ref[rows, :] = blk.astype(BF16)
        else:
            for dst, srcs, width in regroup:
                parts = [blk[:, src:src + width] for src in srcs]
                val = parts[0] if len(parts) == 1 else jnp.concatenate(parts, axis=1)
                dst_ref[rows, dst:dst + len(srcs) * width] = val.astype(BF16)
        return carry

    lax.fori_loop(0, n_chunks, body, 0)


def _rms(x, g):
    return x * lax.rsqrt(jnp.mean(x * x, axis=-1, keepdims=True) + EPS) * g


def _head_slot(shape):
    return lax.broadcasted_iota(jnp.int32, shape, 1) // HEAD_DIM


def _slot_masks(rows):
    slot = _head_slot((rows, QUAD))
    return [(slot == j).astype(BF16) for j in range(QUAD // HEAD_DIM)]


STAGE_DELAY_A = (0, 3, 3)
STAGE_DELAY_B = (0, 1, 1)


def _run_pipelined(units, delays):
    state = {}
    for t in range(len(units) + delays[-1]):
        for k, delay in enumerate(delays):
            u = t - delay
            if 0 <= u < len(units):
                state[u] = units[u][k](state.get(u))


def _pair_unit(k_all, q2, vt2, second_row, sinks, biases, store):
    def scores(_):
        st = lax.dot_general(k_all(), q2(), _NT, preferred_element_type=F32)
        sts, ms = [], []
        for p in range(2):
            sth = st[:, p * LANES:(p + 1) * LANES]
            if biases[p] is not None:
                sth = sth + biases[p]()
            m = jnp.max(sth, axis=0, keepdims=True)
            if sinks[p] is not None:
                m = jnp.maximum(m, sinks[p])
            sts.append(sth)
            ms.append(m)
        return sts, ms

    def softmax(state):
        sts, ms = state
        es = [jnp.exp2(sth - m).astype(BF16) for sth, m in zip(sts, ms)]
        return jnp.concatenate(es, axis=1), ms

    def weighted_values(state):
        e2, ms = state
        v = vt2()
        n_v = v.shape[0]
        lhs = jnp.concatenate([v, jnp.ones((16, v.shape[1]), BF16)], axis=0)
        ot = jnp.dot(lhs, e2, preferred_element_type=F32)
        outs = []
        for p in range(2):
            l = ot[n_v:n_v + 1, p * LANES:(p + 1) * LANES]
            if sinks[p] is not None:
                l = l + jnp.exp2(sinks[p] - ms[p])
            r0 = p * second_row
            outs.append(ot[r0:r0 + HEAD_DIM, p * LANES:(p + 1) * LANES] * (1.0 / l))
        store(jnp.concatenate(outs, axis=0))

    return scores, softmax, weighted_values


def _mem_units(qm_ref, rows, mkv_ref, mvt_ref, masks, store_pair):
    def q2(t):
        qm = qm_ref[rows, :]
        return jnp.concatenate([qm * masks[2 * t], qm * masks[2 * t + 1]], axis=0)

    return [_pair_unit(lambda: mkv_ref[:, :MW], functools.partial(q2, t),
                       lambda t=t: mvt_ref[t * LANES:(t + 1) * LANES, :], HEAD_DIM,
                       (None, None), (None, None), functools.partial(store_pair, t))
            for t in range(N_MEM_HEADS // 2)]


def _mem_kv_kernel(mem_ref, g_ref, w_ref, o_ref):
    mem_n = _rms(mem_ref[...], g_ref[...])
    o_ref[...] = jnp.dot(mem_n.astype(BF16), w_ref[...],
                         preferred_element_type=F32).astype(BF16)


def _mem_kv(mem, norm_mem, w_mem_kv):
    depth, d, w = w_mem_kv.shape
    n_mem = mem.shape[0]
    return pl.pallas_call(
        _mem_kv_kernel,
        grid=(depth,),
        in_specs=[_resident((n_mem, d)), _resident((1, d)),
                  pl.BlockSpec((None, d, w), lambda i: (i, 0, 0))],
        out_specs=pl.BlockSpec((None, n_mem, w), lambda i: (i, 0, 0)),
        out_shape=jax.ShapeDtypeStruct((depth, n_mem, w), BF16),
        name="mem_kv",
    )(mem, norm_mem.reshape(1, d), w_mem_kv)


def _query_col_scale(width, n_q, n_qm):
    cs = np.ones((1, width), np.float32)
    cs[:, :n_q] = Q_SCALE
    cs[:, width - n_qm:] = Q_SCALE
    return jnp.asarray(cs)


def _proj_kernel(x_ref, g_ref, w_hbm, cs_ref, o_ref, w_ref, stage_ref, sem_ref, *, layer, regroup):
    @pl.when(pl.program_id(0) == 0)
    def _():
        _load_cast(w_hbm, layer, w_ref, stage_ref, sem_ref, regroup)

    h = _rms(x_ref[...], g_ref[...]).astype(BF16)
    o_ref[...] = (jnp.dot(h, w_ref[...], preferred_element_type=F32) * cs_ref[...]).astype(BF16)


def _proj(x, g, layer, w, w_layer, regroup):
    s, d = x.shape
    n = w.shape[2]
    return pl.pallas_call(
        functools.partial(_proj_kernel, layer=w_layer, regroup=regroup),
        grid=(s // ROW_TILE,),
        in_specs=[pl.BlockSpec((ROW_TILE, d), lambda i: (i, 0)),
                  _layer_block(g, layer), pl.BlockSpec(memory_space=pl.ANY), _resident((1, n))],
        out_specs=pl.BlockSpec((ROW_TILE, n), lambda i: (i, 0)),
        out_shape=jax.ShapeDtypeStruct((s, n), BF16),
        scratch_shapes=_weight_scratch(w),
        compiler_params=pltpu.CompilerParams(
            dimension_semantics=("arbitrary",), vmem_limit_bytes=VMEM_LIMIT_BYTES),
        name="in_proj",
    )(x, g, w, _query_col_scale(n, QW, MW))


def _alibi_slope(h):
    return 2.0 ** (-8.0 * (h + 1) / N_Q_HEADS)


def _q_regroup_moves_a(width):
    heads = [GROUP_A * j + g for g in range(GROUP_A) for j in range(N_KV_HEADS_A)]
    moves = [(t * LANES, [heads[2 * t] * HEAD_DIM, heads[2 * t + 1] * HEAD_DIM], HEAD_DIM)
             for t in range(N_Q_HEADS // 2)]
    return moves + [(QW, [QW], width - QW)]


def _slab_block_a(i, nb):
    return jnp.clip(i * SUB_BLOCKS_A - 1, 0, nb - (SUB_BLOCKS_A + 2))


def _attn_a_kernel(sink_ref, q_ref, qm_ref, kv_ref, mkv_ref, y_ref, bias_ref, mvt_ref, vbt_ref,
                   *, nb):
    i = pl.program_id(0)
    band = 3 * BLOCK_A

    @pl.when(i == 0)
    def _():
        c = lax.broadcasted_iota(jnp.int32, (band, BLOCK_A), 0)
        r = lax.broadcasted_iota(jnp.int32, (band, BLOCK_A), 1)
        for var in range(3):
            dist = jnp.abs(r - c + var * BLOCK_A)
            allowed = dist <= WINDOW
            distf = dist.astype(F32)
            for h in range(N_Q_HEADS):
                bias_ref[var, h] = jnp.where(allowed, -(_alibi_slope(h) * LOG2E) * distf, NEG_INF)

        mvt_ref[...] = mkv_ref[:, MW:].T

    slab_blk = _slab_block_a(i, nb)
    masks = _slot_masks(BLOCK_A)

    def store_pair(rows, t, z):
        y_ref[rows, t * LANES:(t + 1) * LANES] = z.T.astype(BF16)

    def token_unit(s, t, off, var):
        rows = slice(s * BLOCK_A, (s + 1) * BLOCK_A)
        heads = (2 * t, 2 * t + 1)
        kvh = [h // GROUP_A for h in heads]

        def k_all():
            if t == 0:
                vbt_ref[s] = kv_ref[pl.ds(off, band), KVW_A:].T
            return kv_ref[pl.ds(off, band), :KVW_A]

        def q2():
            return jnp.concatenate(
                [q_ref[rows, (h % GROUP_A) * QUAD:(h % GROUP_A + 1) * QUAD] * masks[j]
                 for h, j in zip(heads, kvh)], axis=0)

        return _pair_unit(
            k_all, q2, lambda: vbt_ref[s, kvh[0] * HEAD_DIM:(kvh[1] + 1) * HEAD_DIM, :],
            (kvh[1] - kvh[0]) * HEAD_DIM, [sink_ref[h] * LOG2E for h in heads],
            [lambda h=h: bias_ref[var, h] for h in heads],
            functools.partial(store_pair, rows, t))

    units = []
    for s in range(SUB_BLOCKS_A):
        g_blk = i * SUB_BLOCKS_A + s
        band_blk = jnp.clip(g_blk - 1, 0, nb - 3)
        off = pl.multiple_of((band_blk - slab_blk) * BLOCK_A, BLOCK_A)
        var = g_blk - band_blk
        rows = slice(s * BLOCK_A, (s + 1) * BLOCK_A)
        units += [token_unit(s, t, off, var) for t in range(N_Q_HEADS // 2)]
        units += _mem_units(qm_ref, rows, mkv_ref, mvt_ref, masks,
                            lambda t, z, rows=rows: store_pair(rows, N_Q_HEADS // 2 + t, z))
    _run_pipelined(units, STAGE_DELAY_A)


def _attn_a(proj, mkv, layer, sink):
    seq, _ = proj.shape
    nb = seq // BLOCK_A
    tq = SUB_BLOCKS_A * BLOCK_A
    slab = (SUB_BLOCKS_A + 2) * BLOCK_A
    return pl.pallas_call(
        functools.partial(_attn_a_kernel, nb=nb),
        grid=(nb // SUB_BLOCKS_A,),
        in_specs=[
            pl.BlockSpec(memory_space=pltpu.SMEM),
            pl.BlockSpec((tq, QW), lambda i: (i, 0)),
            pl.BlockSpec((tq, MW), lambda i: (i, (QW + 2 * KVW_A) // MW)),
            pl.BlockSpec((pl.Element(slab), pl.Element(2 * KVW_A)),
                         lambda i: (_slab_block_a(i, nb) * BLOCK_A, QW)),
            _layer_block(mkv, layer),
        ],
        out_specs=pl.BlockSpec((tq, QW + MW), lambda i: (i, 0)),
        out_shape=jax.ShapeDtypeStruct((seq, QW + MW), BF16),
        scratch_shapes=[pltpu.VMEM((3, N_Q_HEADS, 3 * BLOCK_A, BLOCK_A), F32),
                        pltpu.VMEM((MW, mkv.shape[1]), BF16),
                        pltpu.VMEM((SUB_BLOCKS_A, KVW_A, 3 * BLOCK_A), BF16)],
        compiler_params=pltpu.CompilerParams(
            dimension_semantics=("arbitrary",), vmem_limit_bytes=VMEM_LIMIT_BYTES),
        name="attn_a",
    )(sink, proj, proj, proj, mkv)


N_DR = 2 * NA_ROWS - 1
N_DC = 2 * NA_COLS - 1
N_QUADS = N_Q_HEADS // 4


def _bias_table_kernel(rpb_ref, o_ref):
    li = pl.program_id(0)
    hq = pl.program_id(1)
    kc = lax.broadcasted_iota(jnp.int32, (GRID_W, QUAD), 0)
    lane = lax.broadcasted_iota(jnp.int32, (GRID_W, QUAD), 1)
    qc = lane % GRID_W
    col_start = jnp.clip(qc - NA_COLS // 2, 0, GRID_W - NA_COLS)
    in_win = (kc >= col_start) & (kc < col_start + NA_COLS)
    idx = jnp.clip(kc - qc, -(NA_COLS - 1), NA_COLS - 1) + NA_COLS - 1
    slot_row = _head_slot((1, QUAD))
    base = (li * N_Q_HEADS + 4 * hq) * (N_DR * N_DC)

    def body(d, carry):
        acc = jnp.zeros((GRID_W, QUAD), F32)
        for v in range(N_DC):
            a = [rpb_ref[base + hl * (N_DR * N_DC) + d * N_DC + v] for hl in range(4)]
            val = jnp.where(slot_row == 0, a[0],
                            jnp.where(slot_row == 1, a[1], jnp.where(slot_row == 2, a[2], a[3])))
            acc = jnp.where(idx == v, val, acc)
        o_ref[pl.ds(pl.multiple_of(d * GRID_W, GRID_W), GRID_W), :] = jnp.where(in_win, acc * LOG2E, NEG_INF)
        return carry

    lax.fori_loop(0, N_DR, body, 0)


def _bias_tables(rpb_b):
    n_layers = rpb_b.shape[0]
    return pl.pallas_call(
        _bias_table_kernel,
        grid=(n_layers, N_QUADS),
        in_specs=[pl.BlockSpec(memory_space=pltpu.SMEM)],
        out_specs=pl.BlockSpec((None, None, N_DR * GRID_W, QUAD),
                               lambda li, hq: (li, hq, 0, 0)),
        out_shape=jax.ShapeDtypeStruct((n_layers, N_QUADS, N_DR * GRID_W, QUAD), F32),
        name="rpb_table",
    )(rpb_b.reshape(-1))


def _slab_start_row(b, rows):
    return jnp.clip(b * ROWS_PER_STEP_B - NA_ROWS // 2, 0, rows - 2 * ROWS_PER_STEP_B)


def _attn_b_kernel(q_ref, qm_ref, kv_ref, bias_ref, mkv_ref, y_ref, mvt_ref, *, rows):
    b = pl.program_id(0)
    win = NA_ROWS * GRID_W
    slot = _head_slot((GRID_W, QUAD))
    slab0 = _slab_start_row(b, rows)

    @pl.when(b == 0)
    def _():
        mvt_ref[...] = mkv_ref[:, MW:].T

    masks64 = _slot_masks(GRID_W)
    masks128 = _slot_masks(LANES)

    def quad_unit(j, hq):
        r = b * ROWS_PER_STEP_B + j
        row_start = jnp.clip(r - NA_ROWS // 2, 0, rows - NA_ROWS)
        koff = pl.multiple_of((row_start - slab0) * GRID_W, GRID_W)
        boff = pl.multiple_of((row_start - r + NA_ROWS - 1) * GRID_W, GRID_W)
        qrows = slice(j * GRID_W, (j + 1) * GRID_W)
        lo = hq * QUAD

        def scores(_):
            q4 = q_ref[qrows, lo:lo + QUAD]
            qd = jnp.concatenate([q4 * masks64[hl] for hl in range(4)], axis=0)
            k4 = kv_ref[pl.ds(koff, win), lo:lo + QUAD]
            st = lax.dot_general(k4, qd, _NT, preferred_element_type=F32)
            st = st + bias_ref[hq, pl.ds(boff, win), :]
            return st, jnp.max(st, axis=0, keepdims=True)

        def softmax(state):
            st, m = state
            e = jnp.exp2(st - m)
            return e.astype(BF16), jnp.sum(e, axis=0, keepdims=True)

        def weighted_values(state):
            p, l = state
            v4 = kv_ref[pl.ds(koff, win), QW + lo:QW + lo + QUAD]
            o2 = lax.dot_general(p, v4, _TN, preferred_element_type=F32)
            inv = jnp.broadcast_to(1.0 / l, (LANES, QUAD)).T
            y4 = o2[:GRID_W] * jnp.tile(inv[:GRID_W], (1, 2))
            for hl in range(1, 4):
                part = o2[hl * GRID_W:(hl + 1) * GRID_W] * jnp.tile(inv[hl * GRID_W:(hl + 1) * GRID_W], (1, 2))
                y4 = jnp.where(slot == hl, part, y4)
            y_ref[qrows, lo:lo + QUAD] = y4.astype(BF16)

        return scores, softmax, weighted_values

    def store_mem_pair(blk, t, z):
        y_ref[blk, QW + t * LANES:QW + (t + 1) * LANES] = z.T.astype(BF16)

    units = [quad_unit(j, hq) for j in range(ROWS_PER_STEP_B) for hq in range(N_QUADS)]
    for s in range(ROWS_PER_STEP_B * GRID_W // LANES):
        blk = slice(s * LANES, (s + 1) * LANES)
        units += _mem_units(qm_ref, blk, mkv_ref, mvt_ref, masks128,
                            functools.partial(store_mem_pair, blk))
    _run_pipelined(units, STAGE_DELAY_B)


def _attn_b(proj, mkv, layer, bias, bias_layer):
    seq, _ = proj.shape
    rows = seq // GRID_W
    tq = ROWS_PER_STEP_B * GRID_W
    slab = 2 * ROWS_PER_STEP_B * GRID_W
    return pl.pallas_call(
        functools.partial(_attn_b_kernel, rows=rows),
        grid=(rows // ROWS_PER_STEP_B,),
        in_specs=[
            pl.BlockSpec((tq, QW), lambda b: (b, 0)),
            pl.BlockSpec((tq, MW), lambda b: (b, 3 * QW // MW)),
            pl.BlockSpec((pl.Element(slab), pl.Element(2 * QW)),
                         lambda b: (_slab_start_row(b, rows) * GRID_W, QW)),
            _layer_block(bias, bias_layer),
            _layer_block(mkv, layer),
        ],
        out_specs=pl.BlockSpec((tq, QW + MW), lambda b: (b, 0)),
        out_shape=jax.ShapeDtypeStruct((seq, QW + MW), BF16),
        scratch_shapes=[pltpu.VMEM((MW, mkv.shape[1]), BF16)],
        compiler_params=pltpu.CompilerParams(
            dimension_semantics=("arbitrary",), vmem_limit_bytes=VMEM_LIMIT_BYTES),
        name="attn_b",
    )(proj, proj, proj, bias, mkv)


def _post_kernel(x_ref, y_ref, wo_hbm, gf_ref, wgu_hbm, wd_hbm, gn_ref, *rest,
                 d_ff, layer, next_w_layer, regroup):
    last = next_w_layer is None
    if last:
        o_ref, *scratch = rest
    else:
        wn_hbm, cs_ref, o_ref, p_ref, *scratch = rest
    wo_ref, wgu_ref, wd_ref = scratch[0], scratch[3], scratch[6]

    @pl.when(pl.program_id(0) == 0)
    def _():
        _load_cast(wo_hbm, layer, *scratch[0:3])
        _load_cast(wgu_hbm, layer, *scratch[3:6])
        _load_cast(wd_hbm, layer, *scratch[6:9])
        if not last:
            _load_cast(wn_hbm, next_w_layer, *scratch[9:12], regroup)

    for half in range(ROW_TILE // SUB_TILE):
        rows = slice(half * SUB_TILE, (half + 1) * SUB_TILE)
        x1 = x_ref[rows, :] + jnp.dot(y_ref[rows, :], wo_ref[...], preferred_element_type=F32)
        h = _rms(x1, gf_ref[...]).astype(BF16)
        gu = jnp.dot(h, wgu_ref[...], preferred_element_type=F32)
        g = gu[:, :d_ff]
        u = gu[:, d_ff:]
        a = (g * (1.0 / (1.0 + jnp.exp(-g))) * u).astype(BF16)
        x2 = x1 + jnp.dot(a, wd_ref[...], preferred_element_type=F32)
        if last:
            o_ref[rows, :] = _rms(x2, gn_ref[...])
        else:
            o_ref[rows, :] = x2
            hn = _rms(x2, gn_ref[...]).astype(BF16)
            p_ref[rows, :] = (jnp.dot(hn, scratch[9][...], preferred_element_type=F32)
                              * cs_ref[...]).astype(BF16)


def _post(x, y, layer, w_out, g_ffn, w_gu, w_d, g_next, next_layer, w_next, w_next_layer, regroup):
    s, d = x.shape
    d_ff = w_d.shape[1]
    last = w_next is None
    row = lambda i: (i, 0)
    hbm = pl.BlockSpec(memory_space=pl.ANY)
    in_specs = [pl.BlockSpec((ROW_TILE, d), row), pl.BlockSpec((ROW_TILE, y.shape[1]), row),
                hbm, _layer_block(g_ffn, layer), hbm, hbm, _layer_block(g_next, next_layer)]
    args = [x, y, w_out, g_ffn, w_gu, w_d, g_next]
    out_specs = [pl.BlockSpec((ROW_TILE, d), row)]
    out_shape = [jax.ShapeDtypeStruct((s, d), F32)]
    scratch = _weight_scratch(w_out) + _weight_scratch(w_gu) + _weight_scratch(w_d)
    if not last:
        n = w_next.shape[2]
        in_specs += [hbm, _resident((1, n))]
        args += [w_next, _query_col_scale(n, QW, MW)]
        out_specs.append(pl.BlockSpec((ROW_TILE, n), row))
        out_shape.append(jax.ShapeDtypeStruct((s, n), BF16))
        scratch += _weight_scratch(w_next)
    outs = pl.pallas_call(
        functools.partial(_post_kernel, d_ff=d_ff, layer=layer,
                          next_w_layer=None if last else w_next_layer, regroup=regroup),
        grid=(s // ROW_TILE,),
        in_specs=in_specs,
        out_specs=out_specs,
        out_shape=out_shape,
        scratch_shapes=scratch,
        compiler_params=pltpu.CompilerParams(
            dimension_semantics=("arbitrary",), vmem_limit_bytes=VMEM_LIMIT_BYTES),
        name="post_last" if last else "post",
    )(*args)
    return (outs[0], None) if last else (outs[0], outs[1])


def kernel(x, mem, norm_mix, norm_ffn, norm_mem, norm_final, w_in_a, sink_a, w_in_b, rpb_b,
           w_mem_kv, w_out, w_gate_up, w_down):
    batch, seq, d = x.shape
    assert batch == 1 and mem.shape[0] == 1
    assert seq % (ROWS_PER_STEP_B * GRID_W) == 0 and seq % ROW_TILE == 0
    assert seq % (SUB_BLOCKS_A * BLOCK_A) == 0 and seq // BLOCK_A >= SUB_BLOCKS_A + 2
    depth = w_out.shape[0]
    xs = x.reshape(seq, d)
    w_in = (w_in_a, w_in_b)
    regroup = (_q_regroup_moves_a(w_in_a.shape[2]), None)
    g_mix = norm_mix.reshape(depth, 1, d)
    g_ffn = norm_ffn.reshape(depth, 1, d)
    g_final = norm_final.reshape(1, 1, d)

    mkv = _mem_kv(mem.reshape(mem.shape[1], d), norm_mem, w_mem_kv.astype(BF16))
    bias = _bias_tables(rpb_b)

    proj = _proj(xs, g_mix, 0, w_in[0], 0, regroup[0])
    for i in range(depth):
        if i % 2 == 0:
            y = _attn_a(proj, mkv, i, sink_a[i // 2])
        else:
            y = _attn_b(proj, mkv, i, bias, i // 2)
        if i == depth - 1:
            xs, proj = _post(xs, y, i, w_out, g_ffn, w_gate_up, w_down, g_final, 0, None, 0, None)
        else:
            nxt = (i + 1) % 2
            xs, proj = _post(xs, y, i, w_out, g_ffn, w_gate_up, w_down, g_mix, i + 1,
                             w_in[nxt], (i + 1) // 2, regroup[nxt])
    return xs.reshape(batch, seq, d)
```

```python
import functools

import numpy as np

import jax
import jax.numpy as jnp
from jax import lax
from jax.experimental import pallas as pl
from jax.experimental.pallas import tpu as pltpu

F32 = jnp.float32
BF16 = jnp.bfloat16

HEAD_DIM = 64
N_Q_HEADS = 12
N_KV_HEADS_A = 4
GROUP_A = N_Q_HEADS // N_KV_HEADS_A
N_MEM_HEADS = 4
WINDOW = 128
BLOCK_A = 128
SUB_BLOCKS_A = 4
GRID_W = 64
NA_ROWS = 8
NA_COLS = 16
ROWS_PER_STEP_B = 8
EPS = 1e-6
NEG_INF = -1e30
LOG2E = 1.4426950408889634
Q_SCALE = HEAD_DIM ** -0.5 * LOG2E

LANES = 128
BF16_ROWS = 16
QW = N_Q_HEADS * HEAD_DIM
KVW_A = N_KV_HEADS_A * HEAD_DIM
MW = N_MEM_HEADS * HEAD_DIM
QUAD = 4 * HEAD_DIM

VMEM_LIMIT_BYTES = 56 * 1024 * 1024
ROW_TILE = 512
SUB_TILE = 256

_NT = (((1,), (1,)), ((), ()))
_TN = (((0,), (0,)), ((), ()))


def _resident(shape):
    n = len(shape)
    return pl.BlockSpec(shape, lambda *_: (0,) * n)


def _layer_block(arr, layer):
    n = arr.ndim - 1
    return pl.BlockSpec((None,) + arr.shape[1:], lambda *_: (layer,) + (0,) * n)


class _CastJob:
    def __init__(self, w, layer, n_steps, regroup=None):
        _, n_rows, n_cols = w.shape
        self.regroup = regroup
        self.n_chunks = n_steps
        while n_rows % (self.n_chunks * BF16_ROWS):
            self.n_chunks //= 2
        rc = n_rows // self.n_chunks
        last = self.n_chunks - 1
        self.in_spec = pl.BlockSpec((None, rc, n_cols), lambda s: (layer, jnp.minimum(s, last), 0))
        self.out_spec = pl.BlockSpec((rc, n_cols), lambda s: (jnp.minimum(s, last), 0))
        self.out_shape = jax.ShapeDtypeStruct((n_rows, n_cols), BF16)


def _cast_chunk(step, job, src_ref, dst_ref):
    @pl.when(step < job.n_chunks)
    def _():
        blk = src_ref[...]
        if job.regroup is None:
            dst_ref[...] = blk.astype(BF16)
        else:
            for dst, srcs, width in job.regroup:
                parts = [blk[:, src:src + width] for src in srcs]
                val = parts[0] if len(parts) == 1 else jnp.concatenate(parts, axis=1)
                dst_ref[:, dst:dst + len(srcs) * width] = val.astype(BF16)


def _rms(x, g):
    return x * lax.rsqrt(jnp.mean(x * x, axis=-1, keepdims=True) + EPS) * g


def _head_slot(shape):
    return lax.broadcasted_iota(jnp.int32, shape, 1) // HEAD_DIM


def _slot_masks(rows):
    slot = _head_slot((rows, QUAD))
    return [(slot == j).astype(BF16) for j in range(QUAD // HEAD_DIM)]


STAGE_DELAY_A = (0, 3, 3)
STAGE_DELAY_B = (0, 1, 1)


def _run_pipelined(units, delays):
    state = {}
    for t in range(len(units) + delays[-1]):
        for k, delay in enumerate(delays):
            u = t - delay
            if 0 <= u < len(units):
                state[u] = units[u][k](state.get(u))


def _pair_unit(k_all, q2, vt2, second_row, sinks, biases, store):
    def scores(_):
        st = lax.dot_general(k_all(), q2(), _NT, preferred_element_type=F32)
        sts, ms = [], []
        for p in range(2):
            sth = st[:, p * LANES:(p + 1) * LANES]
            if biases[p] is not None:
                sth = sth + biases[p]()
            m = jnp.max(sth, axis=0, keepdims=True)
            if sinks[p] is not None:
                m = jnp.maximum(m, sinks[p])
            sts.append(sth)
            ms.append(m)
        return sts, ms

    def softmax(state):
        sts, ms = state
        es = [jnp.exp2(sth - m).astype(BF16) for sth, m in zip(sts, ms)]
        return jnp.concatenate(es, axis=1), ms

    def weighted_values(state):
        e2, ms = state
        v = vt2()
        n_v = v.shape[0]
        lhs = jnp.concatenate([v, jnp.ones((BF16_ROWS, v.shape[1]), BF16)], axis=0)
        ot = jnp.dot(lhs, e2, preferred_element_type=F32)
        outs = []
        for p in range(2):
            l = ot[n_v:n_v + 1, p * LANES:(p + 1) * LANES]
            if sinks[p] is not None:
                l = l + jnp.exp2(sinks[p] - ms[p])
            r0 = p * second_row
            outs.append(ot[r0:r0 + HEAD_DIM, p * LANES:(p + 1) * LANES] * (1.0 / l))
        store(jnp.concatenate(outs, axis=0))

    return scores, softmax, weighted_values


def _mem_units(qm_ref, rows, mkv_ref, mvt_ref, masks, store_pair):
    def q2(t):
        qm = qm_ref[rows, :]
        return jnp.concatenate([qm * masks[2 * t], qm * masks[2 * t + 1]], axis=0)

    return [_pair_unit(lambda: mkv_ref[:, :MW], functools.partial(q2, t),
                       lambda t=t: mvt_ref[t * LANES:(t + 1) * LANES, :], HEAD_DIM,
                       (None, None), (None, None), functools.partial(store_pair, t))
            for t in range(N_MEM_HEADS // 2)]


def _mem_kv_kernel(mem_ref, g_ref, w_ref, w0_ref, o_ref, w0_out_ref, *, job):
    mem_n = _rms(mem_ref[...], g_ref[...])
    o_ref[...] = jnp.dot(mem_n.astype(BF16), w_ref[...].astype(BF16),
                         preferred_element_type=F32).astype(BF16)
    _cast_chunk(pl.program_id(0), job, w0_ref, w0_out_ref)


def _mem_kv(mem, norm_mem, w_mem_kv, w_in0, regroup0):
    depth, d, w = w_mem_kv.shape
    n_mem = mem.shape[0]
    job = _CastJob(w_in0, 0, depth, regroup0)
    return pl.pallas_call(
        functools.partial(_mem_kv_kernel, job=job),
        grid=(depth,),
        in_specs=[_resident((n_mem, d)), _resident((1, d)),
                  pl.BlockSpec((None, d, w), lambda i: (i, 0, 0)), job.in_spec],
        out_specs=[pl.BlockSpec((None, n_mem, w), lambda i: (i, 0, 0)), job.out_spec],
        out_shape=[jax.ShapeDtypeStruct((depth, n_mem, w), BF16), job.out_shape],
        name="mem_kv",
    )(mem, norm_mem.reshape(1, d), w_mem_kv, w_in0)


def _query_col_scale(width, n_q, n_qm):
    cs = np.ones((1, width), np.float32)
    cs[:, :n_q] = Q_SCALE
    cs[:, width - n_qm:] = Q_SCALE
    return jnp.asarray(cs)


def _proj_kernel(x_ref, g_ref, w_ref, cs_ref, o_ref):
    h = _rms(x_ref[...], g_ref[...]).astype(BF16)
    o_ref[...] = (jnp.dot(h, w_ref[...], preferred_element_type=F32) * cs_ref[...]).astype(BF16)


def _proj(x, g, layer, w):
    s, d = x.shape
    n = w.shape[1]
    return pl.pallas_call(
        _proj_kernel,
        grid=(s // ROW_TILE,),
        in_specs=[pl.BlockSpec((ROW_TILE, d), lambda i: (i, 0)),
                  _layer_block(g, layer), _resident(w.shape), _resident((1, n))],
        out_specs=pl.BlockSpec((ROW_TILE, n), lambda i: (i, 0)),
        out_shape=jax.ShapeDtypeStruct((s, n), BF16),
        compiler_params=pltpu.CompilerParams(
            dimension_semantics=("arbitrary",), vmem_limit_bytes=VMEM_LIMIT_BYTES),
        name="in_proj",
    )(x, g, w, _query_col_scale(n, QW, MW))


def _alibi_slope(h):
    return 2.0 ** (-8.0 * (h + 1) / N_Q_HEADS)


def _q_regroup_moves_a(width):
    heads = [GROUP_A * j + g for g in range(GROUP_A) for j in range(N_KV_HEADS_A)]
    moves = [(t * LANES, [heads[2 * t] * HEAD_DIM, heads[2 * t + 1] * HEAD_DIM], HEAD_DIM)
             for t in range(N_Q_HEADS // 2)]
    return moves + [(QW, [QW], width - QW)]


def _slab_block_a(i, nb):
    return jnp.clip(i * SUB_BLOCKS_A - 1, 0, nb - (SUB_BLOCKS_A + 2))


def _attn_a_kernel(sink_ref, q_ref, qm_ref, kv_ref, mkv_ref, *rest, nb, jobs):
    n = len(jobs)
    y_ref, bias_ref, mvt_ref, vbt_ref = rest[n], rest[2 * n + 1], rest[2 * n + 2], rest[2 * n + 3]
    i = pl.program_id(0)
    band = 3 * BLOCK_A
    for job, src_ref, dst_ref in zip(jobs, rest[:n], rest[n + 1:2 * n + 1]):
        _cast_chunk(i, job, src_ref, dst_ref)

    @pl.when(i == 0)
    def _():
        c = lax.broadcasted_iota(jnp.int32, (band, BLOCK_A), 0)
        r = lax.broadcasted_iota(jnp.int32, (band, BLOCK_A), 1)
        for var in range(3):
            dist = jnp.abs(r - c + var * BLOCK_A)
            allowed = dist <= WINDOW
            distf = dist.astype(F32)
            for h in range(N_Q_HEADS):
                bias_ref[var, h] = jnp.where(allowed, -(_alibi_slope(h) * LOG2E) * distf, NEG_INF)

        mvt_ref[...] = mkv_ref[:, MW:].T

    slab_blk = _slab_block_a(i, nb)
    masks = _slot_masks(BLOCK_A)

    def store_pair(rows, t, z):
        y_ref[rows, t * LANES:(t + 1) * LANES] = z.T.astype(BF16)

    def token_unit(s, t, off, var):
        rows = slice(s * BLOCK_A, (s + 1) * BLOCK_A)
        heads = (2 * t, 2 * t + 1)
        kvh = [h // GROUP_A for h in heads]

        def k_all():
            if t == 0:
                vbt_ref[s] = kv_ref[pl.ds(off, band), KVW_A:].T
            return kv_ref[pl.ds(off, band), :KVW_A]

        def q2():
            return jnp.concatenate(
                [q_ref[rows, (h % GROUP_A) * QUAD:(h % GROUP_A + 1) * QUAD] * masks[j]
                 for h, j in zip(heads, kvh)], axis=0)

        return _pair_unit(
            k_all, q2, lambda: vbt_ref[s, kvh[0] * HEAD_DIM:(kvh[1] + 1) * HEAD_DIM, :],
            (kvh[1] - kvh[0]) * HEAD_DIM, [sink_ref[h] * LOG2E for h in heads],
            [lambda h=h: bias_ref[var, h] for h in heads],
            functools.partial(store_pair, rows, t))

    units = []
    for s in range(SUB_BLOCKS_A):
        g_blk = i * SUB_BLOCKS_A + s
        band_blk = jnp.clip(g_blk - 1, 0, nb - 3)
        off = pl.multiple_of((band_blk - slab_blk) * BLOCK_A, BLOCK_A)
        var = g_blk - band_blk
        rows = slice(s * BLOCK_A, (s + 1) * BLOCK_A)
        units += [token_unit(s, t, off, var) for t in range(N_Q_HEADS // 2)]
        units += _mem_units(qm_ref, rows, mkv_ref, mvt_ref, masks,
                            lambda t, z, rows=rows: store_pair(rows, N_Q_HEADS // 2 + t, z))
    _run_pipelined(units, STAGE_DELAY_A)


def _attn_a(proj, mkv, layer, sink, casts):
    seq, _ = proj.shape
    nb = seq // BLOCK_A
    tq = SUB_BLOCKS_A * BLOCK_A
    slab = (SUB_BLOCKS_A + 2) * BLOCK_A
    n_steps = nb // SUB_BLOCKS_A
    jobs = [_CastJob(w, l, n_steps, rg) for w, l, rg in casts]
    outs = pl.pallas_call(
        functools.partial(_attn_a_kernel, nb=nb, jobs=jobs),
        grid=(n_steps,),
        in_specs=[
            pl.BlockSpec(memory_space=pltpu.SMEM),
            pl.BlockSpec((tq, QW), lambda i: (i, 0)),
            pl.BlockSpec((tq, MW), lambda i: (i, (QW + 2 * KVW_A) // MW)),
            pl.BlockSpec((pl.Element(slab), pl.Element(2 * KVW_A)),
                         lambda i: (_slab_block_a(i, nb) * BLOCK_A, QW)),
            _layer_block(mkv, layer),
        ] + [job.in_spec for job in jobs],
        out_specs=[pl.BlockSpec((tq, QW + MW), lambda i: (i, 0))] + [job.out_spec for job in jobs],
        out_shape=[jax.ShapeDtypeStruct((seq, QW + MW), BF16)] + [job.out_shape for job in jobs],
        scratch_shapes=[pltpu.VMEM((3, N_Q_HEADS, 3 * BLOCK_A, BLOCK_A), F32),
                        pltpu.VMEM((MW, mkv.shape[1]), BF16),
                        pltpu.VMEM((SUB_BLOCKS_A, KVW_A, 3 * BLOCK_A), BF16)],
        compiler_params=pltpu.CompilerParams(
            dimension_semantics=("arbitrary",), vmem_limit_bytes=VMEM_LIMIT_BYTES),
        name="attn_a",
    )(sink, proj, proj, proj, mkv, *[w for w, _, _ in casts])
    return outs[0], outs[1:]


N_DR = 2 * NA_ROWS - 1
N_DC = 2 * NA_COLS - 1
N_QUADS = N_Q_HEADS // 4


def _bias_table_kernel(rpb_ref, o_ref):
    li = pl.program_id(0)
    hq = pl.program_id(1)
    kc = lax.broadcasted_iota(jnp.int32, (GRID_W, QUAD), 0)
    lane = lax.broadcasted_iota(jnp.int32, (GRID_W, QUAD), 1)
    qc = lane % GRID_W
    col_start = jnp.clip(qc - NA_COLS // 2, 0, GRID_W - NA_COLS)
    in_win = (kc >= col_start) & (kc < col_start + NA_COLS)
    idx = jnp.clip(kc - qc, -(NA_COLS - 1), NA_COLS - 1) + NA_COLS - 1
    slot_row = _head_slot((1, QUAD))
    base = (li * N_Q_HEADS + 4 * hq) * (N_DR * N_DC)

    def body(d, carry):
        acc = jnp.zeros((GRID_W, QUAD), F32)
        for v in range(N_DC):
            a = [rpb_ref[base + hl * (N_DR * N_DC) + d * N_DC + v] for hl in range(4)]
            val = jnp.where(slot_row == 0, a[0],
                            jnp.where(slot_row == 1, a[1], jnp.where(slot_row == 2, a[2], a[3])))
            acc = jnp.where(idx == v, val, acc)
        o_ref[pl.ds(pl.multiple_of(d * GRID_W, GRID_W), GRID_W), :] = jnp.where(in_win, acc * LOG2E, NEG_INF)
        return carry

    lax.fori_loop(0, N_DR, body, 0)


def _bias_tables(rpb_b):
    n_layers = rpb_b.shape[0]
    return pl.pallas_call(
        _bias_table_kernel,
        grid=(n_layers, N_QUADS),
        in_specs=[pl.BlockSpec(memory_space=pltpu.SMEM)],
        out_specs=pl.BlockSpec((None, None, N_DR * GRID_W, QUAD),
                               lambda li, hq: (li, hq, 0, 0)),
        out_shape=jax.ShapeDtypeStruct((n_layers, N_QUADS, N_DR * GRID_W, QUAD), F32),
        name="rpb_table",
    )(rpb_b.reshape(-1))


def _slab_start_row(b, rows):
    return jnp.clip(b * ROWS_PER_STEP_B - NA_ROWS // 2, 0, rows - 2 * ROWS_PER_STEP_B)


def _attn_b_kernel(q_ref, qm_ref, kv_ref, bias_ref, mkv_ref, *rest, rows, jobs):
    n = len(jobs)
    y_ref, mvt_ref = rest[n], rest[2 * n + 1]
    b = pl.program_id(0)
    for job, src_ref, dst_ref in zip(jobs, rest[:n], rest[n + 1:2 * n + 1]):
        _cast_chunk(b, job, src_ref, dst_ref)
    win = NA_ROWS * GRID_W
    slot = _head_slot((GRID_W, QUAD))
    slab0 = _slab_start_row(b, rows)

    @pl.when(b == 0)
    def _():
        mvt_ref[...] = mkv_ref[:, MW:].T

    masks64 = _slot_masks(GRID_W)
    masks128 = _slot_masks(LANES)

    def quad_unit(j, hq):
        r = b * ROWS_PER_STEP_B + j
        row_start = jnp.clip(r - NA_ROWS // 2, 0, rows - NA_ROWS)
        koff = pl.multiple_of((row_start - slab0) * GRID_W, GRID_W)
        boff = pl.multiple_of((row_start - r + NA_ROWS - 1) * GRID_W, GRID_W)
        qrows = slice(j * GRID_W, (j + 1) * GRID_W)
        lo = hq * QUAD

        def scores(_):
            q4 = q_ref[qrows, lo:lo + QUAD]
            qd = jnp.concatenate([q4 * masks64[hl] for hl in range(4)], axis=0)
            k4 = kv_ref[pl.ds(koff, win), lo:lo + QUAD]
            st = lax.dot_general(k4, qd, _NT, preferred_element_type=F32)
            st = st + bias_ref[hq, pl.ds(boff, win), :]
            return st, jnp.max(st, axis=0, keepdims=True)

        def softmax(state):
            st, m = state
            e = jnp.exp2(st - m)
            return e.astype(BF16), jnp.sum(e, axis=0, keepdims=True)

        def weighted_values(state):
            p, l = state
            v4 = kv_ref[pl.ds(koff, win), QW + lo:QW + lo + QUAD]
            o2 = lax.dot_general(p, v4, _TN, preferred_element_type=F32)
            inv = jnp.broadcast_to(1.0 / l, (LANES, QUAD)).T
            y4 = o2[:GRID_W] * jnp.tile(inv[:GRID_W], (1, 2))
            for hl in range(1, 4):
                part = o2[hl * GRID_W:(hl + 1) * GRID_W] * jnp.tile(inv[hl * GRID_W:(hl + 1) * GRID_W], (1, 2))
                y4 = jnp.where(slot == hl, part, y4)
            y_ref[qrows, lo:lo + QUAD] = y4.astype(BF16)

        return scores, softmax, weighted_values

    def store_mem_pair(blk, t, z):
        y_ref[blk, QW + t * LANES:QW + (t + 1) * LANES] = z.T.astype(BF16)

    units = [quad_unit(j, hq) for j in range(ROWS_PER_STEP_B) for hq in range(N_QUADS)]
    for s in range(ROWS_PER_STEP_B * GRID_W // LANES):
        blk = slice(s * LANES, (s + 1) * LANES)
        units += _mem_units(qm_ref, blk, mkv_ref, mvt_ref, masks128,
                            functools.partial(store_mem_pair, blk))
    _run_pipelined(units, STAGE_DELAY_B)


def _attn_b(proj, mkv, layer, bias, bias_layer, casts):
    seq, _ = proj.shape
    rows = seq // GRID_W
    tq = ROWS_PER_STEP_B * GRID_W
    slab = 2 * ROWS_PER_STEP_B * GRID_W
    n_steps = rows // ROWS_PER_STEP_B
    jobs = [_CastJob(w, l, n_steps, rg) for w, l, rg in casts]
    outs = pl.pallas_call(
        functools.partial(_attn_b_kernel, rows=rows, jobs=jobs),
        grid=(n_steps,),
        in_specs=[
            pl.BlockSpec((tq, QW), lambda b: (b, 0)),
            pl.BlockSpec((tq, MW), lambda b: (b, 3 * QW // MW)),
            pl.BlockSpec((pl.Element(slab), pl.Element(2 * QW)),
                         lambda b: (_slab_start_row(b, rows) * GRID_W, QW)),
            _layer_block(bias, bias_layer),
            _layer_block(mkv, layer),
        ] + [job.in_spec for job in jobs],
        out_specs=[pl.BlockSpec((tq, QW + MW), lambda b: (b, 0))] + [job.out_spec for job in jobs],
        out_shape=[jax.ShapeDtypeStruct((seq, QW + MW), BF16)] + [job.out_shape for job in jobs],
        scratch_shapes=[pltpu.VMEM((MW, mkv.shape[1]), BF16)],
        compiler_params=pltpu.CompilerParams(
            dimension_semantics=("arbitrary",), vmem_limit_bytes=VMEM_LIMIT_BYTES),
        name="attn_b",
    )(proj, proj, proj, bias, mkv, *[w for w, _, _ in casts])
    return outs[0], outs[1:]


def _post_kernel(x_ref, y_ref, wo_ref, gf_ref, wgu_ref, wd_ref, gn_ref, *rest, d_ff, last):
    for half in range(ROW_TILE // SUB_TILE):
        rows = slice(half * SUB_TILE, (half + 1) * SUB_TILE)
        x1 = x_ref[rows, :] + jnp.dot(y_ref[rows, :], wo_ref[...], preferred_element_type=F32)
        h = _rms(x1, gf_ref[...]).astype(BF16)
        gu = jnp.dot(h, wgu_ref[...], preferred_element_type=F32)
        g = gu[:, :d_ff]
        u = gu[:, d_ff:]
        a = (g * (1.0 / (1.0 + jnp.exp(-g))) * u).astype(BF16)
        x2 = x1 + jnp.dot(a, wd_ref[...], preferred_element_type=F32)
        if last:
            (o_ref,) = rest
            o_ref[rows, :] = _rms(x2, gn_ref[...])
        else:
            wn_ref, cs_ref, o_ref, p_ref = rest
            o_ref[rows, :] = x2
            hn = _rms(x2, gn_ref[...]).astype(BF16)
            p_ref[rows, :] = (jnp.dot(hn, wn_ref[...], preferred_element_type=F32)
                              * cs_ref[...]).astype(BF16)


def _post(x, y, w_out, g_ffn, layer, w_gu, w_d, g_next, next_layer, w_next):
    s, d = x.shape
    d_ff = w_d.shape[0]
    last = w_next is None
    row = lambda i: (i, 0)
    in_specs = [pl.BlockSpec((ROW_TILE, d), row), pl.BlockSpec((ROW_TILE, y.shape[1]), row),
                _resident(w_out.shape), _layer_block(g_ffn, layer), _resident(w_gu.shape),
                _resident(w_d.shape), _layer_block(g_next, next_layer)]
    args = [x, y, w_out, g_ffn, w_gu, w_d, g_next]
    out_specs = [pl.BlockSpec((ROW_TILE, d), row)]
    out_shape = [jax.ShapeDtypeStruct((s, d), F32)]
    if not last:
        n = w_next.shape[1]
        in_specs += [_resident(w_next.shape), _resident((1, n))]
        args += [w_next, _query_col_scale(n, QW, MW)]
        out_specs.append(pl.BlockSpec((ROW_TILE, n), row))
        out_shape.append(jax.ShapeDtypeStruct((s, n), BF16))
    outs = pl.pallas_call(
        functools.partial(_post_kernel, d_ff=d_ff, last=last),
        grid=(s // ROW_TILE,),
        in_specs=in_specs,
        out_specs=out_specs,
        out_shape=out_shape,
        compiler_params=pltpu.CompilerParams(
            dimension_semantics=("arbitrary",), vmem_limit_bytes=VMEM_LIMIT_BYTES),
        name="post_last" if last else "post",
    )(*args)
    return (outs[0], None) if last else (outs[0], outs[1])


def kernel(x, mem, norm_mix, norm_ffn, norm_mem, norm_final, w_in_a, sink_a, w_in_b, rpb_b,
           w_mem_kv, w_out, w_gate_up, w_down):
    batch, seq, d = x.shape
    assert batch == 1 and mem.shape[0] == 1
    assert seq % (ROWS_PER_STEP_B * GRID_W) == 0 and seq % ROW_TILE == 0
    assert seq % (SUB_BLOCKS_A * BLOCK_A) == 0 and seq // BLOCK_A >= SUB_BLOCKS_A + 2
    depth = w_out.shape[0]
    xs = x.reshape(seq, d)
    w_in = (w_in_a, w_in_b)
    regroup = (_q_regroup_moves_a(w_in_a.shape[2]), None)
    g_mix = norm_mix.reshape(depth, 1, d)
    g_ffn = norm_ffn.reshape(depth, 1, d)
    g_final = norm_final.reshape(1, 1, d)

    mkv, w_in0 = _mem_kv(mem.reshape(mem.shape[1], d), norm_mem, w_mem_kv, w_in[0], regroup[0])
    bias = _bias_tables(rpb_b)

    proj = _proj(xs, g_mix, 0, w_in0)
    for i in range(depth):
        last = i == depth - 1
        nxt = (i + 1) % 2
        casts = [(w_out, i, None), (w_gate_up, i, None), (w_down, i, None)]
        if not last:
            casts.append((w_in[nxt], (i + 1) // 2, regroup[nxt]))
        if i % 2 == 0:
            y, wb = _attn_a(proj, mkv, i, sink_a[i // 2], casts)
        else:
            y, wb = _attn_b(proj, mkv, i, bias, i // 2, casts)
        if last:
            xs, proj = _post(xs, y, wb[0], g_ffn, i, wb[1], wb[2], g_final, 0, None)
        else:
            xs, proj = _post(xs, y, wb[0], g_ffn, i, wb[1], wb[2], g_mix, i + 1, wb[3])
    return xs.reshape(batch, seq, d)
```

```python
import functools

import numpy as np

import jax
import jax.numpy as jnp
from jax import lax
from jax.experimental import pallas as pl
from jax.experimental.pallas import tpu as pltpu

F32 = jnp.float32
BF16 = jnp.bfloat16

HEAD_DIM = 64
N_Q_HEADS = 12
N_KV_HEADS_A = 4
GROUP_A = N_Q_HEADS // N_KV_HEADS_A
N_MEM_HEADS = 4
WINDOW = 128
BLOCK_A = 128
SUB_BLOCKS_A = 4
GRID_W = 64
NA_ROWS = 8
NA_COLS = 16
ROWS_PER_STEP_B = 8
EPS = 1e-6
NEG_INF = -1e30
LOG2E = 1.4426950408889634
Q_SCALE = HEAD_DIM ** -0.5 * LOG2E

LANES = 128
BF16_ROWS = 16
QW = N_Q_HEADS * HEAD_DIM
KVW_A = N_KV_HEADS_A * HEAD_DIM
MW = N_MEM_HEADS * HEAD_DIM
QUAD = 4 * HEAD_DIM

VMEM_LIMIT_BYTES = 56 * 1024 * 1024
ROW_TILE = 512
SUB_TILE = 256

_NT = (((1,), (1,)), ((), ()))
_TN = (((0,), (0,)), ((), ()))


def _resident(shape):
    n = len(shape)
    return pl.BlockSpec(shape, lambda *_: (0,) * n)


def _layer_block(arr, layer):
    n = arr.ndim - 1
    return pl.BlockSpec((None,) + arr.shape[1:], lambda *_: (layer,) + (0,) * n)


class _CastJob:
    def __init__(self, w, layer, n_steps, regroup=None):
        _, n_rows, n_cols = w.shape
        self.regroup = regroup
        self.n_chunks = n_steps
        while n_rows % (self.n_chunks * BF16_ROWS):
            self.n_chunks //= 2
        rc = n_rows // self.n_chunks
        last = self.n_chunks - 1
        self.in_spec = pl.BlockSpec((None, rc, n_cols), lambda s: (layer, jnp.minimum(s, last), 0))
        self.out_spec = pl.BlockSpec((rc, n_cols), lambda s: (jnp.minimum(s, last), 0))
        self.out_shape = jax.ShapeDtypeStruct((n_rows, n_cols), BF16)


def _cast_chunk(step, job, src_ref, dst_ref):
    @pl.when(step < job.n_chunks)
    def _():
        blk = src_ref[...]
        if job.regroup is None:
            dst_ref[...] = blk.astype(BF16)
        else:
            for dst, srcs, width in job.regroup:
                parts = [blk[:, src:src + width] for src in srcs]
                val = parts[0] if len(parts) == 1 else jnp.concatenate(parts, axis=1)
                dst_ref[:, dst:dst + len(srcs) * width] = val.astype(BF16)


def _rms(x, g):
    return x * lax.rsqrt(jnp.mean(x * x, axis=-1, keepdims=True) + EPS) * g


def _head_slot(shape):
    return lax.broadcasted_iota(jnp.int32, shape, 1) // HEAD_DIM


def _slot_masks(rows):
    slot = _head_slot((rows, QUAD))
    return [(slot == j).astype(BF16) for j in range(QUAD // HEAD_DIM)]


STAGE_DELAY_A = (0, 2)
STAGE_DELAY_B = (0, 1, 1)


def _run_pipelined(units, delays):
    state = {}
    for t in range(len(units) + delays[-1]):
        for k, delay in enumerate(delays):
            u = t - delay
            if 0 <= u < len(units):
                stages = units[u]
                if len(stages) == len(delays):
                    state[u] = stages[k](state.get(u))
                elif k in (0, len(delays) - 1):
                    state[u] = stages[min(k, 1)](state.get(u))


def _pair_unit(k_all, q2, vt2, second_row, sinks, biases, store):
    def scores(_):
        return lax.dot_general(k_all(), q2(), _NT, preferred_element_type=F32)

    def weighted_values(st):
        v = vt2()
        n_v = v.shape[0]
        lhs = jnp.concatenate([v, jnp.ones((BF16_ROWS, v.shape[1]), BF16)], axis=0)
        sink_row = None
        if sinks[0] is not None:
            sink_row = jnp.concatenate([jnp.full((1, LANES), s_, F32) for s_ in sinks], axis=1)
        m = sink_row if sink_row is not None else jnp.full((1, 2 * LANES), NEG_INF, F32)
        acc = None
        for c in range(st.shape[0] // LANES):
            keys = slice(c * LANES, (c + 1) * LANES)
            s_c = st[keys, :]
            if biases[0] is not None:
                s_c = s_c + jnp.concatenate([b(keys) for b in biases], axis=1)
            m_new = jnp.maximum(m, jnp.max(s_c, axis=0, keepdims=True))
            e = jnp.exp2(s_c - m_new).astype(BF16)
            part = jnp.dot(lhs[:, keys], e, preferred_element_type=F32)
            acc = part if acc is None else acc * jnp.exp2(m - m_new) + part
            m = m_new
        l = acc[n_v:n_v + 1, :]
        if sink_row is not None:
            l = l + jnp.exp2(sink_row - m)
        inv = 1.0 / l
        outs = []
        for p in range(2):
            cols = slice(p * LANES, (p + 1) * LANES)
            r0 = p * second_row
            outs.append(acc[r0:r0 + HEAD_DIM, cols] * inv[:, cols])
        store(jnp.concatenate(outs, axis=0))

    return scores, weighted_values


def _mem_units(qm_ref, rows, mkv_ref, mvt_ref, masks, store_pair):
    def q2(t):
        qm = qm_ref[rows, :]
        return jnp.concatenate([qm * masks[2 * t], qm * masks[2 * t + 1]], axis=0)

    return [_pair_unit(lambda: mkv_ref[:, :MW], functools.partial(q2, t),
                       lambda t=t: mvt_ref[t * LANES:(t + 1) * LANES, :], HEAD_DIM,
                       (None, None), (None, None), functools.partial(store_pair, t))
            for t in range(N_MEM_HEADS // 2)]


def _mem_kv_kernel(mem_ref, g_ref, w_ref, w0_ref, o_ref, w0_out_ref, *, job):
    mem_n = _rms(mem_ref[...], g_ref[...])
    o_ref[...] = jnp.dot(mem_n.astype(BF16), w_ref[...].astype(BF16),
                         preferred_element_type=F32).astype(BF16)
    _cast_chunk(pl.program_id(0), job, w0_ref, w0_out_ref)


def _mem_kv(mem, norm_mem, w_mem_kv, w_in0, regroup0):
    depth, d, w = w_mem_kv.shape
    n_mem = mem.shape[0]
    job = _CastJob(w_in0, 0, depth, regroup0)
    return pl.pallas_call(
        functools.partial(_mem_kv_kernel, job=job),
        grid=(depth,),
        in_specs=[_resident((n_mem, d)), _resident((1, d)),
                  pl.BlockSpec((None, d, w), lambda i: (i, 0, 0)), job.in_spec],
        out_specs=[pl.BlockSpec((None, n_mem, w), lambda i: (i, 0, 0)), job.out_spec],
        out_shape=[jax.ShapeDtypeStruct((depth, n_mem, w), BF16), job.out_shape],
        name="mem_kv",
    )(mem, norm_mem.reshape(1, d), w_mem_kv, w_in0)


def _query_col_scale(width, n_q, n_qm):
    cs = np.ones((1, width), np.float32)
    cs[:, :n_q] = Q_SCALE
    cs[:, width - n_qm:] = Q_SCALE
    return jnp.asarray(cs)


def _proj_kernel(x_ref, g_ref, w_ref, cs_ref, o_ref):
    h = _rms(x_ref[...], g_ref[...]).astype(BF16)
    o_ref[...] = (jnp.dot(h, w_ref[...], preferred_element_type=F32) * cs_ref[...]).astype(BF16)


def _proj(x, g, layer, w):
    s, d = x.shape
    n = w.shape[1]
    return pl.pallas_call(
        _proj_kernel,
        grid=(s // ROW_TILE,),
        in_specs=[pl.BlockSpec((ROW_TILE, d), lambda i: (i, 0)),
                  _layer_block(g, layer), _resident(w.shape), _resident((1, n))],
        out_specs=pl.BlockSpec((ROW_TILE, n), lambda i: (i, 0)),
        out_shape=jax.ShapeDtypeStruct((s, n), BF16),
        compiler_params=pltpu.CompilerParams(
            dimension_semantics=("arbitrary",), vmem_limit_bytes=VMEM_LIMIT_BYTES),
        name="in_proj",
    )(x, g, w, _query_col_scale(n, QW, MW))


def _alibi_slope(h):
    return 2.0 ** (-8.0 * (h + 1) / N_Q_HEADS)


def _q_regroup_moves_a(width):
    heads = [GROUP_A * j + g for g in range(GROUP_A) for j in range(N_KV_HEADS_A)]
    moves = [(t * LANES, [heads[2 * t] * HEAD_DIM, heads[2 * t + 1] * HEAD_DIM], HEAD_DIM)
             for t in range(N_Q_HEADS // 2)]
    return moves + [(QW, [QW], width - QW)]


def _slab_block_a(i, nb):
    return jnp.clip(i * SUB_BLOCKS_A - 1, 0, nb - (SUB_BLOCKS_A + 2))


def _attn_a_kernel(sink_ref, q_ref, qm_ref, kv_ref, mkv_ref, *rest, nb, jobs):
    n = len(jobs)
    y_ref, bias_ref, mvt_ref, vbt_ref = rest[n], rest[2 * n + 1], rest[2 * n + 2], rest[2 * n + 3]
    i = pl.program_id(0)
    band = 3 * BLOCK_A
    for job, src_ref, dst_ref in zip(jobs, rest[:n], rest[n + 1:2 * n + 1]):
        _cast_chunk(i, job, src_ref, dst_ref)

    @pl.when(i == 0)
    def _():
        c = lax.broadcasted_iota(jnp.int32, (band, BLOCK_A), 0)
        r = lax.broadcasted_iota(jnp.int32, (band, BLOCK_A), 1)
        for var in range(3):
            dist = jnp.abs(r - c + var * BLOCK_A)
            allowed = dist <= WINDOW
            distf = dist.astype(F32)
            for h in range(N_Q_HEADS):
                bias_ref[var, h] = jnp.where(allowed, -(_alibi_slope(h) * LOG2E) * distf, NEG_INF)

        mvt_ref[...] = mkv_ref[:, MW:].T

    slab_blk = _slab_block_a(i, nb)
    masks = _slot_masks(BLOCK_A)

    def store_pair(rows, t, z):
        y_ref[rows, t * LANES:(t + 1) * LANES] = z.T.astype(BF16)

    def token_unit(s, t, off, var):
        rows = slice(s * BLOCK_A, (s + 1) * BLOCK_A)
        heads = (2 * t, 2 * t + 1)
        kvh = [h // GROUP_A for h in heads]

        def k_all():
            if t == 0:
                vbt_ref[s] = kv_ref[pl.ds(off, band), KVW_A:].T
            return kv_ref[pl.ds(off, band), :KVW_A]

        def q2():
            return jnp.concatenate(
                [q_ref[rows, (h % GROUP_A) * QUAD:(h % GROUP_A + 1) * QUAD] * masks[j]
                 for h, j in zip(heads, kvh)], axis=0)

        return _pair_unit(
            k_all, q2, lambda: vbt_ref[s, kvh[0] * HEAD_DIM:(kvh[1] + 1) * HEAD_DIM, :],
            (kvh[1] - kvh[0]) * HEAD_DIM, [sink_ref[h] * LOG2E for h in heads],
            [lambda keys, h=h: bias_ref[var, h, keys, :] for h in heads],
            functools.partial(store_pair, rows, t))

    units = []
    for s in range(SUB_BLOCKS_A):
        g_blk = i * SUB_BLOCKS_A + s
        band_blk = jnp.clip(g_blk - 1, 0, nb - 3)
        off = pl.multiple_of((band_blk - slab_blk) * BLOCK_A, BLOCK_A)
        var = g_blk - band_blk
        rows = slice(s * BLOCK_A, (s + 1) * BLOCK_A)
        units += [token_unit(s, t, off, var) for t in range(N_Q_HEADS // 2)]
        units += _mem_units(qm_ref, rows, mkv_ref, mvt_ref, masks,
                            lambda t, z, rows=rows: store_pair(rows, N_Q_HEADS // 2 + t, z))
    _run_pipelined(units, STAGE_DELAY_A)


def _attn_a(proj, mkv, layer, sink, casts):
    seq, _ = proj.shape
    nb = seq // BLOCK_A
    tq = SUB_BLOCKS_A * BLOCK_A
    slab = (SUB_BLOCKS_A + 2) * BLOCK_A
    n_steps = nb // SUB_BLOCKS_A
    jobs = [_CastJob(w, l, n_steps, rg) for w, l, rg in casts]
    outs = pl.pallas_call(
        functools.partial(_attn_a_kernel, nb=nb, jobs=jobs),
        grid=(n_steps,),
        in_specs=[
            pl.BlockSpec(memory_space=pltpu.SMEM),
            pl.BlockSpec((tq, QW), lambda i: (i, 0)),
            pl.BlockSpec((tq, MW), lambda i: (i, (QW + 2 * KVW_A) // MW)),
            pl.BlockSpec((pl.Element(slab), pl.Element(2 * KVW_A)),
                         lambda i: (_slab_block_a(i, nb) * BLOCK_A, QW)),
            _layer_block(mkv, layer),
        ] + [job.in_spec for job in jobs],
        out_specs=[pl.BlockSpec((tq, QW + MW), lambda i: (i, 0))] + [job.out_spec for job in jobs],
        out_shape=[jax.ShapeDtypeStruct((seq, QW + MW), BF16)] + [job.out_shape for job in jobs],
        scratch_shapes=[pltpu.VMEM((3, N_Q_HEADS, 3 * BLOCK_A, BLOCK_A), F32),
                        pltpu.VMEM((MW, mkv.shape[1]), BF16),
                        pltpu.VMEM((SUB_BLOCKS_A, KVW_A, 3 * BLOCK_A), BF16)],
        compiler_params=pltpu.CompilerParams(
            dimension_semantics=("arbitrary",), vmem_limit_bytes=VMEM_LIMIT_BYTES),
        name="attn_a",
    )(sink, proj, proj, proj, mkv, *[w for w, _, _ in casts])
    return outs[0], outs[1:]


N_DR = 2 * NA_ROWS - 1
N_DC = 2 * NA_COLS - 1
N_QUADS = N_Q_HEADS // 4


def _bias_table_kernel(rpb_ref, o_ref):
    li = pl.program_id(0)
    hq = pl.program_id(1)
    kc = lax.broadcasted_iota(jnp.int32, (GRID_W, QUAD), 0)
    lane = lax.broadcasted_iota(jnp.int32, (GRID_W, QUAD), 1)
    qc = lane % GRID_W
    col_start = jnp.clip(qc - NA_COLS // 2, 0, GRID_W - NA_COLS)
    in_win = (kc >= col_start) & (kc < col_start + NA_COLS)
    idx = jnp.clip(kc - qc, -(NA_COLS - 1), NA_COLS - 1) + NA_COLS - 1
    slot_row = _head_slot((1, QUAD))
    base = (li * N_Q_HEADS + 4 * hq) * (N_DR * N_DC)

    def body(d, carry):
        acc = jnp.zeros((GRID_W, QUAD), F32)
        for v in range(N_DC):
            a = [rpb_ref[base + hl * (N_DR * N_DC) + d * N_DC + v] for hl in range(4)]
            val = jnp.where(slot_row == 0, a[0],
                            jnp.where(slot_row == 1, a[1], jnp.where(slot_row == 2, a[2], a[3])))
            acc = jnp.where(idx == v, val, acc)
        o_ref[pl.ds(pl.multiple_of(d * GRID_W, GRID_W), GRID_W), :] = jnp.where(in_win, acc * LOG2E, NEG_INF)
        return carry

    lax.fori_loop(0, N_DR, body, 0)


def _bias_tables(rpb_b):
    n_layers = rpb_b.shape[0]
    return pl.pallas_call(
        _bias_table_kernel,
        grid=(n_layers, N_QUADS),
        in_specs=[pl.BlockSpec(memory_space=pltpu.SMEM)],
        out_specs=pl.BlockSpec((None, None, N_DR * GRID_W, QUAD),
                               lambda li, hq: (li, hq, 0, 0)),
        out_shape=jax.ShapeDtypeStruct((n_layers, N_QUADS, N_DR * GRID_W, QUAD), F32),
        name="rpb_table",
    )(rpb_b.reshape(-1))


def _slab_start_row(b, rows):
    return jnp.clip(b * ROWS_PER_STEP_B - NA_ROWS // 2, 0, rows - 2 * ROWS_PER_STEP_B)


def _attn_b_kernel(q_ref, qm_ref, kv_ref, bias_ref, mkv_ref, *rest, rows, jobs):
    n = len(jobs)
    y_ref, mvt_ref = rest[n], rest[2 * n + 1]
    b = pl.program_id(0)
    for job, src_ref, dst_ref in zip(jobs, rest[:n], rest[n + 1:2 * n + 1]):
        _cast_chunk(b, job, src_ref, dst_ref)
    win = NA_ROWS * GRID_W
    slot = _head_slot((GRID_W, QUAD))
    slab0 = _slab_start_row(b, rows)

    @pl.when(b == 0)
    def _():
        mvt_ref[...] = mkv_ref[:, MW:].T

    masks64 = _slot_masks(GRID_W)
    masks128 = _slot_masks(LANES)

    def quad_unit(j, hq):
        r = b * ROWS_PER_STEP_B + j
        row_start = jnp.clip(r - NA_ROWS // 2, 0, rows - NA_ROWS)
        koff = pl.multiple_of((row_start - slab0) * GRID_W, GRID_W)
        boff = pl.multiple_of((row_start - r + NA_ROWS - 1) * GRID_W, GRID_W)
        qrows = slice(j * GRID_W, (j + 1) * GRID_W)
        lo = hq * QUAD

        def scores(_):
            q4 = q_ref[qrows, lo:lo + QUAD]
            qd = jnp.concatenate([q4 * masks64[hl] for hl in range(4)], axis=0)
            k4 = kv_ref[pl.ds(koff, win), lo:lo + QUAD]
            st = lax.dot_general(k4, qd, _NT, preferred_element_type=F32)
            st = st + bias_ref[hq, pl.ds(boff, win), :]
            return st, jnp.max(st, axis=0, keepdims=True)

        def softmax(state):
            st, m = state
            e = jnp.exp2(st - m)
            return e.astype(BF16), jnp.sum(e, axis=0, keepdims=True)

        def weighted_values(state):
            p, l = state
            v4 = kv_ref[pl.ds(koff, win), QW + lo:QW + lo + QUAD]
            o2 = lax.dot_general(p, v4, _TN, preferred_element_type=F32)
            inv = jnp.broadcast_to(1.0 / l, (LANES, QUAD)).T
            y4 = o2[:GRID_W] * jnp.tile(inv[:GRID_W], (1, 2))
            for hl in range(1, 4):
                part = o2[hl * GRID_W:(hl + 1) * GRID_W] * jnp.tile(inv[hl * GRID_W:(hl + 1) * GRID_W], (1, 2))
                y4 = jnp.where(slot == hl, part, y4)
            y_ref[qrows, lo:lo + QUAD] = y4.astype(BF16)

        return scores, softmax, weighted_values

    def store_mem_pair(blk, t, z):
        y_ref[blk, QW + t * LANES:QW + (t + 1) * LANES] = z.T.astype(BF16)

    units = [quad_unit(j, hq) for j in range(ROWS_PER_STEP_B) for hq in range(N_QUADS)]
    for s in range(ROWS_PER_STEP_B * GRID_W // LANES):
        blk = slice(s * LANES, (s + 1) * LANES)
        units += _mem_units(qm_ref, blk, mkv_ref, mvt_ref, masks128,
                            functools.partial(store_mem_pair, blk))
    _run_pipelined(units, STAGE_DELAY_B)


def _attn_b(proj, mkv, layer, bias, bias_layer, casts):
    seq, _ = proj.shape
    rows = seq // GRID_W
    tq = ROWS_PER_STEP_B * GRID_W
    slab = 2 * ROWS_PER_STEP_B * GRID_W
    n_steps = rows // ROWS_PER_STEP_B
    jobs = [_CastJob(w, l, n_steps, rg) for w, l, rg in casts]
    outs = pl.pallas_call(
        functools.partial(_attn_b_kernel, rows=rows, jobs=jobs),
        grid=(n_steps,),
        in_specs=[
            pl.BlockSpec((tq, QW), lambda b: (b, 0)),
            pl.BlockSpec((tq, MW), lambda b: (b, 3 * QW // MW)),
            pl.BlockSpec((pl.Element(slab), pl.Element(2 * QW)),
                         lambda b: (_slab_start_row(b, rows) * GRID_W, QW)),
            _layer_block(bias, bias_layer),
            _layer_block(mkv, layer),
        ] + [job.in_spec for job in jobs],
        out_specs=[pl.BlockSpec((tq, QW + MW), lambda b: (b, 0))] + [job.out_spec for job in jobs],
        out_shape=[jax.ShapeDtypeStruct((seq, QW + MW), BF16)] + [job.out_shape for job in jobs],
        scratch_shapes=[pltpu.VMEM((MW, mkv.shape[1]), BF16)],
        compiler_params=pltpu.CompilerParams(
            dimension_semantics=("arbitrary",), vmem_limit_bytes=VMEM_LIMIT_BYTES),
        name="attn_b",
    )(proj, proj, proj, bias, mkv, *[w for w, _, _ in casts])
    return outs[0], outs[1:]


def _post_kernel(x_ref, y_ref, wo_ref, gf_ref, wgu_ref, wd_ref, gn_ref, *rest, d_ff, last):
    for half in range(ROW_TILE // SUB_TILE):
        rows = slice(half * SUB_TILE, (half + 1) * SUB_TILE)
        x1 = x_ref[rows, :] + jnp.dot(y_ref[rows, :], wo_ref[...], preferred_element_type=F32)
        h = _rms(x1, gf_ref[...]).astype(BF16)
        gu = jnp.dot(h, wgu_ref[...], preferred_element_type=F32)
        g = gu[:, :d_ff]
        u = gu[:, d_ff:]
        a = (g * (1.0 / (1.0 + jnp.exp(-g))) * u).astype(BF16)
        x2 = x1 + jnp.dot(a, wd_ref[...], preferred_element_type=F32)
        if last:
            (o_ref,) = rest
            o_ref[rows, :] = _rms(x2, gn_ref[...])
        else:
            wn_ref, cs_ref, o_ref, p_ref = rest
            o_ref[rows, :] = x2
            hn = _rms(x2, gn_ref[...]).astype(BF16)
            p_ref[rows, :] = (jnp.dot(hn, wn_ref[...], preferred_element_type=F32)
                              * cs_ref[...]).astype(BF16)


def _post(x, y, w_out, g_ffn, layer, w_gu, w_d, g_next, next_layer, w_next):
    s, d = x.shape
    d_ff = w_d.shape[0]
    last = w_next is None
    row = lambda i: (i, 0)
    in_specs = [pl.BlockSpec((ROW_TILE, d), row), pl.BlockSpec((ROW_TILE, y.shape[1]), row),
                _resident(w_out.shape), _layer_block(g_ffn, layer), _resident(w_gu.shape),
                _resident(w_d.shape), _layer_block(g_next, next_layer)]
    args = [x, y, w_out, g_ffn, w_gu, w_d, g_next]
    out_specs = [pl.BlockSpec((ROW_TILE, d), row)]
    out_shape = [jax.ShapeDtypeStruct((s, d), F32)]
    if not last:
        n = w_next.shape[1]
        in_specs += [_resident(w_next.shape), _resident((1, n))]
        args += [w_next, _query_col_scale(n, QW, MW)]
        out_specs.append(pl.BlockSpec((ROW_TILE, n), row))
        out_shape.append(jax.ShapeDtypeStruct((s, n), BF16))
    outs = pl.pallas_call(
        functools.partial(_post_kernel, d_ff=d_ff, last=last),
        grid=(s // ROW_TILE,),
        in_specs=in_specs,
        out_specs=out_specs,
        out_shape=out_shape,
        compiler_params=pltpu.CompilerParams(
            dimension_semantics=("arbitrary",), vmem_limit_bytes=VMEM_LIMIT_BYTES),
        name="post_last" if last else "post",
    )(*args)
    return (outs[0], None) if last else (outs[0], outs[1])


def kernel(x, mem, norm_mix, norm_ffn, norm_mem, norm_final, w_in_a, sink_a, w_in_b, rpb_b,
           w_mem_kv, w_out, w_gate_up, w_down):
    batch, seq, d = x.shape
    assert batch == 1 and mem.shape[0] == 1
    assert seq % (ROWS_PER_STEP_B * GRID_W) == 0 and seq % ROW_TILE == 0
    assert seq % (SUB_BLOCKS_A * BLOCK_A) == 0 and seq // BLOCK_A >= SUB_BLOCKS_A + 2
    depth = w_out.shape[0]
    xs = x.reshape(seq, d)
    w_in = (w_in_a, w_in_b)
    regroup = (_q_regroup_moves_a(w_in_a.shape[2]), None)
    g_mix = norm_mix.reshape(depth, 1, d)
    g_ffn = norm_ffn.reshape(depth, 1, d)
    g_final = norm_final.reshape(1, 1, d)

    mkv, w_in0 = _mem_kv(mem.reshape(mem.shape[1], d), norm_mem, w_mem_kv, w_in[0], regroup[0])
    bias = _bias_tables(rpb_b)

    proj = _proj(xs, g_mix, 0, w_in0)
    for i in range(depth):
        last = i == depth - 1
        nxt = (i + 1) % 2
        casts = [(w_out, i, None), (w_gate_up, i, None), (w_down, i, None)]
        if not last:
            casts.append((w_in[nxt], (i + 1) // 2, regroup[nxt]))
        if i % 2 == 0:
            y, wb = _attn_a(proj, mkv, i, sink_a[i // 2], casts)
        else:
            y, wb = _attn_b(proj, mkv, i, bias, i // 2, casts)
        if last:
            xs, proj = _post(xs, y, wb[0], g_ffn, i, wb[1], wb[2], g_final, 0, None)
        else:
            xs, proj = _post(xs, y, wb[0], g_ffn, i, wb[1], wb[2], g_mix, i + 1, wb[3])
    return xs.reshape(batch, seq, d)
```

```python
import functools

import numpy as np

import jax
import jax.numpy as jnp
from jax import lax
from jax.experimental import pallas as pl
from jax.experimental.pallas import tpu as pltpu

F32 = jnp.float32
BF16 = jnp.bfloat16

HEAD_DIM = 64
N_Q_HEADS = 12
N_KV_HEADS_A = 4
GROUP_A = N_Q_HEADS // N_KV_HEADS_A
N_MEM_HEADS = 4
WINDOW = 128
BLOCK_A = 128
SUB_BLOCKS_A = 4
GRID_W = 64
NA_ROWS = 8
NA_COLS = 16
ROWS_PER_STEP_B = 8
EPS = 1e-6
NEG_INF = -1e30
LOG2E = 1.4426950408889634
Q_SCALE = HEAD_DIM ** -0.5 * LOG2E

LANES = 128
BF16_ROWS = 16
QW = N_Q_HEADS * HEAD_DIM
KVW_A = N_KV_HEADS_A * HEAD_DIM
MW = N_MEM_HEADS * HEAD_DIM
QUAD = 4 * HEAD_DIM

VMEM_LIMIT_BYTES = 56 * 1024 * 1024
ROW_TILE = 512
SUB_TILE = 256

_NT = (((1,), (1,)), ((), ()))
_TN = (((0,), (0,)), ((), ()))


def _resident(shape):
    n = len(shape)
    return pl.BlockSpec(shape, lambda *_: (0,) * n)


def _layer_block(arr, layer):
    n = arr.ndim - 1
    return pl.BlockSpec((None,) + arr.shape[1:], lambda *_: (layer,) + (0,) * n)


class _CastJob:
    def __init__(self, w, layer, n_steps, regroup=None):
        _, n_rows, n_cols = w.shape
        self.regroup = regroup
        self.n_chunks = n_steps
        while n_rows % (self.n_chunks * BF16_ROWS):
            self.n_chunks //= 2
        rc = n_rows // self.n_chunks
        last = self.n_chunks - 1
        self.in_spec = pl.BlockSpec((None, rc, n_cols), lambda s: (layer, jnp.minimum(s, last), 0))
        self.out_spec = pl.BlockSpec((rc, n_cols), lambda s: (jnp.minimum(s, last), 0))
        self.out_shape = jax.ShapeDtypeStruct((n_rows, n_cols), BF16)


def _cast_chunk(step, job, src_ref, dst_ref):
    @pl.when(step < job.n_chunks)
    def _():
        blk = src_ref[...]
        if job.regroup is None:
            dst_ref[...] = blk.astype(BF16)
        else:
            for dst, srcs, width in job.regroup:
                parts = [blk[:, src:src + width] for src in srcs]
                val = parts[0] if len(parts) == 1 else jnp.concatenate(parts, axis=1)
                dst_ref[:, dst:dst + len(srcs) * width] = val.astype(BF16)


def _rms(x, g):
    return x * lax.rsqrt(jnp.mean(x * x, axis=-1, keepdims=True) + EPS) * g


def _head_slot(shape):
    return lax.broadcasted_iota(jnp.int32, shape, 1) // HEAD_DIM


def _slot_masks(rows):
    slot = _head_slot((rows, QUAD))
    return [(slot == j).astype(BF16) for j in range(QUAD // HEAD_DIM)]


STAGE_DELAY_A = (0, 2)
STAGE_DELAY_B = (0, 1, 1)


def _run_pipelined(units, delays):
    state = {}
    for t in range(len(units) + delays[-1]):
        for k, delay in enumerate(delays):
            u = t - delay
            if 0 <= u < len(units):
                stages = units[u]
                if len(stages) == len(delays):
                    state[u] = stages[k](state.get(u))
                elif k in (0, len(delays) - 1):
                    state[u] = stages[min(k, 1)](state.get(u))


def _pair_unit(k_all, q2, vt2, second_row, sinks, biases, store):
    def scores(_):
        return lax.dot_general(k_all(), q2(), _NT, preferred_element_type=F32)

    def weighted_values(st):
        v = vt2()
        n_v = v.shape[0]
        lhs = jnp.concatenate([v, jnp.ones((BF16_ROWS, v.shape[1]), BF16)], axis=0)
        sink_row = None
        if sinks[0] is not None:
            sink_row = jnp.concatenate([jnp.full((1, LANES), s_, F32) for s_ in sinks], axis=1)
        m = sink_row if sink_row is not None else jnp.full((1, 2 * LANES), NEG_INF, F32)
        acc = None
        for c in range(st.shape[0] // LANES):
            keys = slice(c * LANES, (c + 1) * LANES)
            s_c = st[keys, :]
            if biases[0] is not None:
                s_c = s_c + jnp.concatenate([b(keys) for b in biases], axis=1)
            m_new = jnp.maximum(m, jnp.max(s_c, axis=0, keepdims=True))
            e = jnp.exp2(s_c - m_new).astype(BF16)
            part = jnp.dot(lhs[:, keys], e, preferred_element_type=F32)
            acc = part if acc is None else acc * jnp.exp2(m - m_new) + part
            m = m_new
        l = acc[n_v:n_v + 1, :]
        if sink_row is not None:
            l = l + jnp.exp2(sink_row - m)
        inv = 1.0 / l
        outs = []
        for p in range(2):
            cols = slice(p * LANES, (p + 1) * LANES)
            r0 = p * second_row
            outs.append(acc[r0:r0 + HEAD_DIM, cols] * inv[:, cols])
        store(jnp.concatenate(outs, axis=0))

    return scores, weighted_values


def _mem_units(qm_ref, rows, mkv_ref, mvt_ref, masks, store_pair):
    def q2(t):
        qm = qm_ref[rows, :]
        return jnp.concatenate([qm * masks[2 * t], qm * masks[2 * t + 1]], axis=0)

    return [_pair_unit(lambda: mkv_ref[:, :MW], functools.partial(q2, t),
                       lambda t=t: mvt_ref[t * LANES:(t + 1) * LANES, :], HEAD_DIM,
                       (None, None), (None, None), functools.partial(store_pair, t))
            for t in range(N_MEM_HEADS // 2)]


def _mem_kv_kernel(mem_ref, g_ref, w_ref, w0_ref, o_ref, w0_out_ref, *, job):
    mem_n = _rms(mem_ref[...], g_ref[...])
    o_ref[...] = jnp.dot(mem_n.astype(BF16), w_ref[...].astype(BF16),
                         preferred_element_type=F32).astype(BF16)
    _cast_chunk(pl.program_id(0), job, w0_ref, w0_out_ref)


def _mem_kv(mem, norm_mem, w_mem_kv, w_in0, regroup0):
    depth, d, w = w_mem_kv.shape
    n_mem = mem.shape[0]
    job = _CastJob(w_in0, 0, depth, regroup0)
    return pl.pallas_call(
        functools.partial(_mem_kv_kernel, job=job),
        grid=(depth,),
        in_specs=[_resident((n_mem, d)), _resident((1, d)),
                  pl.BlockSpec((None, d, w), lambda i: (i, 0, 0)), job.in_spec],
        out_specs=[pl.BlockSpec((None, n_mem, w), lambda i: (i, 0, 0)), job.out_spec],
        out_shape=[jax.ShapeDtypeStruct((depth, n_mem, w), BF16), job.out_shape],
        name="mem_kv",
    )(mem, norm_mem.reshape(1, d), w_mem_kv, w_in0)


def _query_col_scale(width, n_q, n_qm):
    cs = np.ones((1, width), np.float32)
    cs[:, :n_q] = Q_SCALE
    cs[:, width - n_qm:] = Q_SCALE
    return jnp.asarray(cs)


def _proj_kernel(x_ref, g_ref, w_ref, cs_ref, o_ref):
    tiles = [slice(k * SUB_TILE, (k + 1) * SUB_TILE) for k in range(ROW_TILE // SUB_TILE)]
    hs = [_rms(x_ref[rows, :], g_ref[...]).astype(BF16) for rows in tiles]
    for h, rows in zip(hs, tiles):
        o_ref[rows, :] = (jnp.dot(h, w_ref[...], preferred_element_type=F32)
                          * cs_ref[...]).astype(BF16)


def _proj(x, g, layer, w):
    s, d = x.shape
    n = w.shape[1]
    return pl.pallas_call(
        _proj_kernel,
        grid=(s // ROW_TILE,),
        in_specs=[pl.BlockSpec((ROW_TILE, d), lambda i: (i, 0)),
                  _layer_block(g, layer), _resident(w.shape), _resident((1, n))],
        out_specs=pl.BlockSpec((ROW_TILE, n), lambda i: (i, 0)),
        out_shape=jax.ShapeDtypeStruct((s, n), BF16),
        compiler_params=pltpu.CompilerParams(
            dimension_semantics=("arbitrary",), vmem_limit_bytes=VMEM_LIMIT_BYTES),
        name="in_proj",
    )(x, g, w, _query_col_scale(n, QW, MW))


def _alibi_slope(h):
    return 2.0 ** (-8.0 * (h + 1) / N_Q_HEADS)


def _q_regroup_moves_a(width):
    heads = [GROUP_A * j + g for g in range(GROUP_A) for j in range(N_KV_HEADS_A)]
    moves = [(t * LANES, [heads[2 * t] * HEAD_DIM, heads[2 * t + 1] * HEAD_DIM], HEAD_DIM)
             for t in range(N_Q_HEADS // 2)]
    return moves + [(QW, [QW], width - QW)]


def _slab_block_a(i, nb):
    return jnp.clip(i * SUB_BLOCKS_A - 1, 0, nb - (SUB_BLOCKS_A + 2))


def _attn_a_kernel(sink_ref, q_ref, qm_ref, kv_ref, mkv_ref, *rest, nb, jobs):
    n = len(jobs)
    y_ref, bias_ref, mvt_ref, vbt_ref = rest[n], rest[2 * n + 1], rest[2 * n + 2], rest[2 * n + 3]
    i = pl.program_id(0)
    band = 3 * BLOCK_A
    for job, src_ref, dst_ref in zip(jobs, rest[:n], rest[n + 1:2 * n + 1]):
        _cast_chunk(i, job, src_ref, dst_ref)

    @pl.when(i == 0)
    def _():
        c = lax.broadcasted_iota(jnp.int32, (band, BLOCK_A), 0)
        r = lax.broadcasted_iota(jnp.int32, (band, BLOCK_A), 1)
        for var in range(3):
            dist = jnp.abs(r - c + var * BLOCK_A)
            allowed = dist <= WINDOW
            distf = dist.astype(F32)
            for h in range(N_Q_HEADS):
                bias_ref[var, h] = jnp.where(allowed, -(_alibi_slope(h) * LOG2E) * distf, NEG_INF)

        mvt_ref[...] = mkv_ref[:, MW:].T

    slab_blk = _slab_block_a(i, nb)
    masks = _slot_masks(BLOCK_A)

    def store_pair(rows, t, z):
        y_ref[rows, t * LANES:(t + 1) * LANES] = z.T.astype(BF16)

    def token_unit(s, t, off, var):
        rows = slice(s * BLOCK_A, (s + 1) * BLOCK_A)
        heads = (2 * t, 2 * t + 1)
        kvh = [h // GROUP_A for h in heads]

        def k_all():
            if t == 0:
                vbt_ref[s] = kv_ref[pl.ds(off, band), KVW_A:].T
            return kv_ref[pl.ds(off, band), :KVW_A]

        def q2():
            return jnp.concatenate(
                [q_ref[rows, (h % GROUP_A) * QUAD:(h % GROUP_A + 1) * QUAD] * masks[j]
                 for h, j in zip(heads, kvh)], axis=0)

        return _pair_unit(
            k_all, q2, lambda: vbt_ref[s, kvh[0] * HEAD_DIM:(kvh[1] + 1) * HEAD_DIM, :],
            (kvh[1] - kvh[0]) * HEAD_DIM, [sink_ref[h] * LOG2E for h in heads],
            [lambda keys, h=h: bias_ref[var, h, keys, :] for h in heads],
            functools.partial(store_pair, rows, t))

    units = []
    for s in range(SUB_BLOCKS_A):
        g_blk = i * SUB_BLOCKS_A + s
        band_blk = jnp.clip(g_blk - 1, 0, nb - 3)
        off = pl.multiple_of((band_blk - slab_blk) * BLOCK_A, BLOCK_A)
        var = g_blk - band_blk
        rows = slice(s * BLOCK_A, (s + 1) * BLOCK_A)
        units += [token_unit(s, t, off, var) for t in range(N_Q_HEADS // 2)]
        units += _mem_units(qm_ref, rows, mkv_ref, mvt_ref, masks,
                            lambda t, z, rows=rows: store_pair(rows, N_Q_HEADS // 2 + t, z))
    _run_pipelined(units, STAGE_DELAY_A)


def _attn_a(proj, mkv, layer, sink, casts):
    seq, _ = proj.shape
    nb = seq // BLOCK_A
    tq = SUB_BLOCKS_A * BLOCK_A
    slab = (SUB_BLOCKS_A + 2) * BLOCK_A
    n_steps = nb // SUB_BLOCKS_A
    jobs = [_CastJob(w, l, n_steps, rg) for w, l, rg in casts]
    outs = pl.pallas_call(
        functools.partial(_attn_a_kernel, nb=nb, jobs=jobs),
        grid=(n_steps,),
        in_specs=[
            pl.BlockSpec(memory_space=pltpu.SMEM),
            pl.BlockSpec((tq, QW), lambda i: (i, 0)),
            pl.BlockSpec((tq, MW), lambda i: (i, (QW + 2 * KVW_A) // MW)),
            pl.BlockSpec((pl.Element(slab), pl.Element(2 * KVW_A)),
                         lambda i: (_slab_block_a(i, nb) * BLOCK_A, QW)),
            _layer_block(mkv, layer),
        ] + [job.in_spec for job in jobs],
        out_specs=[pl.BlockSpec((tq, QW + MW), lambda i: (i, 0))] + [job.out_spec for job in jobs],
        out_shape=[jax.ShapeDtypeStruct((seq, QW + MW), BF16)] + [job.out_shape for job in jobs],
        scratch_shapes=[pltpu.VMEM((3, N_Q_HEADS, 3 * BLOCK_A, BLOCK_A), F32),
                        pltpu.VMEM((MW, mkv.shape[1]), BF16),
                        pltpu.VMEM((SUB_BLOCKS_A, KVW_A, 3 * BLOCK_A), BF16)],
        compiler_params=pltpu.CompilerParams(
            dimension_semantics=("arbitrary",), vmem_limit_bytes=VMEM_LIMIT_BYTES),
        name="attn_a",
    )(sink, proj, proj, proj, mkv, *[w for w, _, _ in casts])
    return outs[0], outs[1:]


N_DR = 2 * NA_ROWS - 1
N_DC = 2 * NA_COLS - 1
N_QUADS = N_Q_HEADS // 4


def _bias_table_kernel(rpb_ref, o_ref):
    li = pl.program_id(0)
    hq = pl.program_id(1)
    kc = lax.broadcasted_iota(jnp.int32, (GRID_W, QUAD), 0)
    lane = lax.broadcasted_iota(jnp.int32, (GRID_W, QUAD), 1)
    qc = lane % GRID_W
    col_start = jnp.clip(qc - NA_COLS // 2, 0, GRID_W - NA_COLS)
    in_win = (kc >= col_start) & (kc < col_start + NA_COLS)
    idx = jnp.clip(kc - qc, -(NA_COLS - 1), NA_COLS - 1) + NA_COLS - 1
    slot_row = _head_slot((1, QUAD))
    base = (li * N_Q_HEADS + 4 * hq) * (N_DR * N_DC)

    def body(d, carry):
        acc = jnp.zeros((GRID_W, QUAD), F32)
        for v in range(N_DC):
            a = [rpb_ref[base + hl * (N_DR * N_DC) + d * N_DC + v] for hl in range(4)]
            val = jnp.where(slot_row == 0, a[0],
                            jnp.where(slot_row == 1, a[1], jnp.where(slot_row == 2, a[2], a[3])))
            acc = jnp.where(idx == v, val, acc)
        o_ref[pl.ds(pl.multiple_of(d * GRID_W, GRID_W), GRID_W), :] = jnp.where(in_win, acc * LOG2E, NEG_INF)
        return carry

    lax.fori_loop(0, N_DR, body, 0)


def _bias_tables(rpb_b):
    n_layers = rpb_b.shape[0]
    return pl.pallas_call(
        _bias_table_kernel,
        grid=(n_layers, N_QUADS),
        in_specs=[pl.BlockSpec(memory_space=pltpu.SMEM)],
        out_specs=pl.BlockSpec((None, None, N_DR * GRID_W, QUAD),
                               lambda li, hq: (li, hq, 0, 0)),
        out_shape=jax.ShapeDtypeStruct((n_layers, N_QUADS, N_DR * GRID_W, QUAD), F32),
        name="rpb_table",
    )(rpb_b.reshape(-1))


def _slab_start_row(b, rows):
    return jnp.clip(b * ROWS_PER_STEP_B - NA_ROWS // 2, 0, rows - 2 * ROWS_PER_STEP_B)


def _attn_b_kernel(q_ref, qm_ref, kv_ref, bias_ref, mkv_ref, *rest, rows, jobs):
    n = len(jobs)
    y_ref, mvt_ref = rest[n], rest[2 * n + 1]
    b = pl.program_id(0)
    for job, src_ref, dst_ref in zip(jobs, rest[:n], rest[n + 1:2 * n + 1]):
        _cast_chunk(b, job, src_ref, dst_ref)
    win = NA_ROWS * GRID_W
    slot = _head_slot((GRID_W, QUAD))
    slab0 = _slab_start_row(b, rows)

    @pl.when(b == 0)
    def _():
        mvt_ref[...] = mkv_ref[:, MW:].T

    masks64 = _slot_masks(GRID_W)
    masks128 = _slot_masks(LANES)

    def quad_unit(j, hq):
        r = b * ROWS_PER_STEP_B + j
        row_start = jnp.clip(r - NA_ROWS // 2, 0, rows - NA_ROWS)
        koff = pl.multiple_of((row_start - slab0) * GRID_W, GRID_W)
        boff = pl.multiple_of((row_start - r + NA_ROWS - 1) * GRID_W, GRID_W)
        qrows = slice(j * GRID_W, (j + 1) * GRID_W)
        lo = hq * QUAD

        def scores(_):
            q4 = q_ref[qrows, lo:lo + QUAD]
            qd = jnp.concatenate([q4 * masks64[hl] for hl in range(4)], axis=0)
            k4 = kv_ref[pl.ds(koff, win), lo:lo + QUAD]
            st = lax.dot_general(k4, qd, _NT, preferred_element_type=F32)
            st = st + bias_ref[hq, pl.ds(boff, win), :]
            return st, jnp.max(st, axis=0, keepdims=True)

        def softmax(state):
            st, m = state
            e = jnp.exp2(st - m)
            return e.astype(BF16), jnp.sum(e, axis=0, keepdims=True)

        def weighted_values(state):
            p, l = state
            v4 = kv_ref[pl.ds(koff, win), QW + lo:QW + lo + QUAD]
            o2 = lax.dot_general(p, v4, _TN, preferred_element_type=F32)
            inv = jnp.broadcast_to(1.0 / l, (LANES, QUAD)).T
            y4 = o2[:GRID_W] * jnp.tile(inv[:GRID_W], (1, 2))
            for hl in range(1, 4):
                part = o2[hl * GRID_W:(hl + 1) * GRID_W] * jnp.tile(inv[hl * GRID_W:(hl + 1) * GRID_W], (1, 2))
                y4 = jnp.where(slot == hl, part, y4)
            y_ref[qrows, lo:lo + QUAD] = y4.astype(BF16)

        return scores, softmax, weighted_values

    def store_mem_pair(blk, t, z):
        y_ref[blk, QW + t * LANES:QW + (t + 1) * LANES] = z.T.astype(BF16)

    units = [quad_unit(j, hq) for j in range(ROWS_PER_STEP_B) for hq in range(N_QUADS)]
    for s in range(ROWS_PER_STEP_B * GRID_W // LANES):
        blk = slice(s * LANES, (s + 1) * LANES)
        units += _mem_units(qm_ref, blk, mkv_ref, mvt_ref, masks128,
                            functools.partial(store_mem_pair, blk))
    _run_pipelined(units, STAGE_DELAY_B)


def _attn_b(proj, mkv, layer, bias, bias_layer, casts):
    seq, _ = proj.shape
    rows = seq // GRID_W
    tq = ROWS_PER_STEP_B * GRID_W
    slab = 2 * ROWS_PER_STEP_B * GRID_W
    n_steps = rows // ROWS_PER_STEP_B
    jobs = [_CastJob(w, l, n_steps, rg) for w, l, rg in casts]
    outs = pl.pallas_call(
        functools.partial(_attn_b_kernel, rows=rows, jobs=jobs),
        grid=(n_steps,),
        in_specs=[
            pl.BlockSpec((tq, QW), lambda b: (b, 0)),
            pl.BlockSpec((tq, MW), lambda b: (b, 3 * QW // MW)),
            pl.BlockSpec((pl.Element(slab), pl.Element(2 * QW)),
                         lambda b: (_slab_start_row(b, rows) * GRID_W, QW)),
            _layer_block(bias, bias_layer),
            _layer_block(mkv, layer),
        ] + [job.in_spec for job in jobs],
        out_specs=[pl.BlockSpec((tq, QW + MW), lambda b: (b, 0))] + [job.out_spec for job in jobs],
        out_shape=[jax.ShapeDtypeStruct((seq, QW + MW), BF16)] + [job.out_shape for job in jobs],
        scratch_shapes=[pltpu.VMEM((MW, mkv.shape[1]), BF16)],
        compiler_params=pltpu.CompilerParams(
            dimension_semantics=("arbitrary",), vmem_limit_bytes=VMEM_LIMIT_BYTES),
        name="attn_b",
    )(proj, proj, proj, bias, mkv, *[w for w, _, _ in casts])
    return outs[0], outs[1:]


def _post_kernel(x_ref, y_ref, wo_ref, gf_ref, wgu_ref, wd_ref, gn_ref, *rest, d_ff, last):
    tiles = [slice(k * SUB_TILE, (k + 1) * SUB_TILE) for k in range(ROW_TILE // SUB_TILE)]
    x1 = [x_ref[rows, :] + jnp.dot(y_ref[rows, :], wo_ref[...], preferred_element_type=F32)
          for rows in tiles]
    act = []
    for k in range(len(tiles)):
        h = _rms(x1[k], gf_ref[...]).astype(BF16)
        gu = jnp.dot(h, wgu_ref[...], preferred_element_type=F32)
        g = gu[:, :d_ff]
        u = gu[:, d_ff:]
        act.append((g * (1.0 / (1.0 + jnp.exp(-g))) * u).astype(BF16))
    x2 = [x1[k] + jnp.dot(act[k], wd_ref[...], preferred_element_type=F32)
          for k in range(len(tiles))]
    if last:
        (o_ref,) = rest
        for k, rows in enumerate(tiles):
            o_ref[rows, :] = _rms(x2[k], gn_ref[...])
    else:
        wn_ref, cs_ref, o_ref, p_ref = rest
        for k, rows in enumerate(tiles):
            o_ref[rows, :] = x2[k]
            hn = _rms(x2[k], gn_ref[...]).astype(BF16)
            p_ref[rows, :] = (jnp.dot(hn, wn_ref[...], preferred_element_type=F32)
                              * cs_ref[...]).astype(BF16)


def _post(x, y, w_out, g_ffn, layer, w_gu, w_d, g_next, next_layer, w_next):
    s, d = x.shape
    d_ff = w_d.shape[0]
    last = w_next is None
    row = lambda i: (i, 0)
    in_specs = [pl.BlockSpec((ROW_TILE, d), row), pl.BlockSpec((ROW_TILE, y.shape[1]), row),
                _resident(w_out.shape), _layer_block(g_ffn, layer), _resident(w_gu.shape),
                _resident(w_d.shape), _layer_block(g_next, next_layer)]
    args = [x, y, w_out, g_ffn, w_gu, w_d, g_next]
    out_specs = [pl.BlockSpec((ROW_TILE, d), row)]
    out_shape = [jax.ShapeDtypeStruct((s, d), F32)]
    if not last:
        n = w_next.shape[1]
        in_specs += [_resident(w_next.shape), _resident((1, n))]
        args += [w_next, _query_col_scale(n, QW, MW)]
        out_specs.append(pl.BlockSpec((ROW_TILE, n), row))
        out_shape.append(jax.ShapeDtypeStruct((s, n), BF16))
    outs = pl.pallas_call(
        functools.partial(_post_kernel, d_ff=d_ff, last=last),
        grid=(s // ROW_TILE,),
        in_specs=in_specs,
        out_specs=out_specs,
        out_shape=out_shape,
        compiler_params=pltpu.CompilerParams(
            dimension_semantics=("arbitrary",), vmem_limit_bytes=VMEM_LIMIT_BYTES),
        name="post_last" if last else "post",
    )(*args)
    return (outs[0], None) if last else (outs[0], outs[1])


def kernel(x, mem, norm_mix, norm_ffn, norm_mem, norm_final, w_in_a, sink_a, w_in_b, rpb_b,
           w_mem_kv, w_out, w_gate_up, w_down):
    batch, seq, d = x.shape
    assert batch == 1 and mem.shape[0] == 1
    assert seq % (ROWS_PER_STEP_B * GRID_W) == 0 and seq % ROW_TILE == 0
    assert seq % (SUB_BLOCKS_A * BLOCK_A) == 0 and seq // BLOCK_A >= SUB_BLOCKS_A + 2
    depth = w_out.shape[0]
    xs = x.reshape(seq, d)
    w_in = (w_in_a, w_in_b)
    regroup = (_q_regroup_moves_a(w_in_a.shape[2]), None)
    g_mix = norm_mix.reshape(depth, 1, d)
    g_ffn = norm_ffn.reshape(depth, 1, d)
    g_final = norm_final.reshape(1, 1, d)

    mkv, w_in0 = _mem_kv(mem.reshape(mem.shape[1], d), norm_mem, w_mem_kv, w_in[0], regroup[0])
    bias = _bias_tables(rpb_b)

    proj = _proj(xs, g_mix, 0, w_in0)
    for i in range(depth):
        last = i == depth - 1
        nxt = (i + 1) % 2
        casts = [(w_out, i, None), (w_gate_up, i, None), (w_down, i, None)]
        if not last:
            casts.append((w_in[nxt], (i + 1) // 2, regroup[nxt]))
        if i % 2 == 0:
            y, wb = _attn_a(proj, mkv, i, sink_a[i // 2], casts)
        else:
            y, wb = _attn_b(proj, mkv, i, bias, i // 2, casts)
        if last:
            xs, proj = _post(xs, y, wb[0], g_ffn, i, wb[1], wb[2], g_final, 0, None)
        else:
            xs, proj = _post(xs, y, wb[0], g_ffn, i, wb[1], wb[2], g_mix, i + 1, wb[3])
    return xs.reshape(batch, seq, d)
```

```python
import functools

import numpy as np

import jax
import jax.numpy as jnp
from jax import lax
from jax.experimental import pallas as pl
from jax.experimental.pallas import tpu as pltpu

F32 = jnp.float32
BF16 = jnp.bfloat16

HEAD_DIM = 64
N_Q_HEADS = 12
N_KV_HEADS_A = 4
GROUP_A = N_Q_HEADS // N_KV_HEADS_A
N_MEM_HEADS = 4
WINDOW = 128
BLOCK_A = 128
SUB_BLOCKS_A = 4
GRID_W = 64
NA_ROWS = 8
NA_COLS = 16
ROWS_PER_STEP_B = 8
EPS = 1e-6
NEG_INF = -1e30
LOG2E = 1.4426950408889634
Q_SCALE = HEAD_DIM ** -0.5 * LOG2E

LANES = 128
BF16_ROWS = 16
QW = N_Q_HEADS * HEAD_DIM
KVW_A = N_KV_HEADS_A * HEAD_DIM
MW = N_MEM_HEADS * HEAD_DIM
QUAD = 4 * HEAD_DIM

VMEM_LIMIT_BYTES = 56 * 1024 * 1024
ROW_TILE = 512
PROJ_TILE = 1024
SUB_TILE = 256

_NT = (((1,), (1,)), ((), ()))
_TN = (((0,), (0,)), ((), ()))


def _resident(shape):
    n = len(shape)
    return pl.BlockSpec(shape, lambda *_: (0,) * n)


def _layer_block(arr, layer):
    n = arr.ndim - 1
    return pl.BlockSpec((None,) + arr.shape[1:], lambda *_: (layer,) + (0,) * n)


class _CastJob:
    def __init__(self, w, layer, n_steps, regroup=None):
        _, n_rows, n_cols = w.shape
        self.regroup = regroup
        self.n_chunks = n_steps
        while n_rows % (self.n_chunks * BF16_ROWS):
            self.n_chunks //= 2
        rc = n_rows // self.n_chunks
        last = self.n_chunks - 1
        self.in_spec = pl.BlockSpec((None, rc, n_cols), lambda s: (layer, jnp.minimum(s, last), 0))
        self.out_spec = pl.BlockSpec((rc, n_cols), lambda s: (jnp.minimum(s, last), 0))
        self.out_shape = jax.ShapeDtypeStruct((n_rows, n_cols), BF16)


def _cast_chunk(step, job, src_ref, dst_ref):
    @pl.when(step < job.n_chunks)
    def _():
        blk = src_ref[...]
        if job.regroup is None:
            dst_ref[...] = blk.astype(BF16)
        else:
            for dst, srcs, width in job.regroup:
                parts = [blk[:, src:src + width] for src in srcs]
                val = parts[0] if len(parts) == 1 else jnp.concatenate(parts, axis=1)
                dst_ref[:, dst:dst + len(srcs) * width] = val.astype(BF16)


def _rms(x, g):
    return x * lax.rsqrt(jnp.mean(x * x, axis=-1, keepdims=True) + EPS) * g


def _head_slot(shape):
    return lax.broadcasted_iota(jnp.int32, shape, 1) // HEAD_DIM


def _slot_masks(rows):
    slot = _head_slot((rows, QUAD))
    return [(slot == j).astype(BF16) for j in range(QUAD // HEAD_DIM)]


STAGE_DELAY_A = (0, 2)
STAGE_DELAY_B = (0, 1, 1)


def _run_pipelined(units, delays):
    state = {}
    for t in range(len(units) + delays[-1]):
        for k, delay in enumerate(delays):
            u = t - delay
            if 0 <= u < len(units):
                stages = units[u]
                if len(stages) == len(delays):
                    state[u] = stages[k](state.get(u))
                elif k in (0, len(delays) - 1):
                    state[u] = stages[min(k, 1)](state.get(u))


def _pair_unit(k_all, q2, vt2, second_row, sinks, biases, store):
    def scores(_):
        return lax.dot_general(k_all(), q2(), _NT, preferred_element_type=F32)

    def weighted_values(st):
        v = vt2()
        n_v = v.shape[0]
        lhs = jnp.concatenate([v, jnp.ones((BF16_ROWS, v.shape[1]), BF16)], axis=0)
        sink_row = None
        if sinks[0] is not None:
            sink_row = jnp.concatenate([jnp.full((1, LANES), s_, F32) for s_ in sinks], axis=1)
        m = sink_row if sink_row is not None else jnp.full((1, 2 * LANES), NEG_INF, F32)
        acc = None
        for c in range(st.shape[0] // LANES):
            keys = slice(c * LANES, (c + 1) * LANES)
            s_c = st[keys, :]
            if biases[0] is not None:
                s_c = s_c + jnp.concatenate([b(keys) for b in biases], axis=1)
            m_new = jnp.maximum(m, jnp.max(s_c, axis=0, keepdims=True))
            e = jnp.exp2(s_c - m_new).astype(BF16)
            part = jnp.dot(lhs[:, keys], e, preferred_element_type=F32)
            acc = part if acc is None else acc * jnp.exp2(m - m_new) + part
            m = m_new
        l = acc[n_v:n_v + 1, :]
        if sink_row is not None:
            l = l + jnp.exp2(sink_row - m)
        inv = 1.0 / l
        outs = []
        for p in range(2):
            cols = slice(p * LANES, (p + 1) * LANES)
            r0 = p * second_row
            outs.append(acc[r0:r0 + HEAD_DIM, cols] * inv[:, cols])
        store(jnp.concatenate(outs, axis=0))

    return scores, weighted_values


def _mem_units(qm_ref, rows, mkv_ref, mvt_ref, masks, store_pair):
    def q2(t):
        qm = qm_ref[rows, :]
        return jnp.concatenate([qm * masks[2 * t], qm * masks[2 * t + 1]], axis=0)

    return [_pair_unit(lambda: mkv_ref[:, :MW], functools.partial(q2, t),
                       lambda t=t: mvt_ref[t * LANES:(t + 1) * LANES, :], HEAD_DIM,
                       (None, None), (None, None), functools.partial(store_pair, t))
            for t in range(N_MEM_HEADS // 2)]


def _mem_kv_kernel(mem_ref, g_ref, w_ref, w0_ref, o_ref, w0_out_ref, *, job):
    mem_n = _rms(mem_ref[...], g_ref[...])
    o_ref[...] = jnp.dot(mem_n.astype(BF16), w_ref[...].astype(BF16),
                         preferred_element_type=F32).astype(BF16)
    _cast_chunk(pl.program_id(0), job, w0_ref, w0_out_ref)


def _mem_kv(mem, norm_mem, w_mem_kv, w_in0, regroup0):
    depth, d, w = w_mem_kv.shape
    n_mem = mem.shape[0]
    job = _CastJob(w_in0, 0, depth, regroup0)
    return pl.pallas_call(
        functools.partial(_mem_kv_kernel, job=job),
        grid=(depth,),
        in_specs=[_resident((n_mem, d)), _resident((1, d)),
                  pl.BlockSpec((None, d, w), lambda i: (i, 0, 0)), job.in_spec],
        out_specs=[pl.BlockSpec((None, n_mem, w), lambda i: (i, 0, 0)), job.out_spec],
        out_shape=[jax.ShapeDtypeStruct((depth, n_mem, w), BF16), job.out_shape],
        name="mem_kv",
    )(mem, norm_mem.reshape(1, d), w_mem_kv, w_in0)


def _query_col_scale(width, n_q, n_qm):
    cs = np.ones((1, width), np.float32)
    cs[:, :n_q] = Q_SCALE
    cs[:, width - n_qm:] = Q_SCALE
    return jnp.asarray(cs)


def _proj_kernel(x_ref, g_ref, w_ref, cs_ref, o_ref):
    tiles = [slice(k * SUB_TILE, (k + 1) * SUB_TILE) for k in range(PROJ_TILE // SUB_TILE)]
    h = _rms(x_ref[tiles[0], :], g_ref[...]).astype(BF16)
    for k, rows in enumerate(tiles):
        h_next = None
        if k + 1 < len(tiles):
            h_next = _rms(x_ref[tiles[k + 1], :], g_ref[...]).astype(BF16)
        o_ref[rows, :] = (jnp.dot(h, w_ref[...], preferred_element_type=F32)
                          * cs_ref[...]).astype(BF16)
        h = h_next


def _proj(x, g, layer, w):
    s, d = x.shape
    n = w.shape[1]
    return pl.pallas_call(
        _proj_kernel,
        grid=(s // PROJ_TILE,),
        in_specs=[pl.BlockSpec((PROJ_TILE, d), lambda i: (i, 0)),
                  _layer_block(g, layer), _resident(w.shape), _resident((1, n))],
        out_specs=pl.BlockSpec((PROJ_TILE, n), lambda i: (i, 0)),
        out_shape=jax.ShapeDtypeStruct((s, n), BF16),
        compiler_params=pltpu.CompilerParams(
            dimension_semantics=("arbitrary",), vmem_limit_bytes=VMEM_LIMIT_BYTES),
        name="in_proj",
    )(x, g, w, _query_col_scale(n, QW, MW))


def _alibi_slope(h):
    return 2.0 ** (-8.0 * (h + 1) / N_Q_HEADS)


def _q_regroup_moves_a(width):
    heads = [GROUP_A * j + g for g in range(GROUP_A) for j in range(N_KV_HEADS_A)]
    moves = [(t * LANES, [heads[2 * t] * HEAD_DIM, heads[2 * t + 1] * HEAD_DIM], HEAD_DIM)
             for t in range(N_Q_HEADS // 2)]
    return moves + [(QW, [QW], width - QW)]


def _slab_block_a(i, nb):
    return jnp.clip(i * SUB_BLOCKS_A - 1, 0, nb - (SUB_BLOCKS_A + 2))


def _attn_a_kernel(sink_ref, q_ref, qm_ref, kv_ref, mkv_ref, *rest, nb, jobs):
    n = len(jobs)
    y_ref, bias_ref, mvt_ref, vbt_ref = rest[n], rest[2 * n + 1], rest[2 * n + 2], rest[2 * n + 3]
    i = pl.program_id(0)
    band = 3 * BLOCK_A
    for job, src_ref, dst_ref in zip(jobs, rest[:n], rest[n + 1:2 * n + 1]):
        _cast_chunk(i, job, src_ref, dst_ref)

    @pl.when(i == 0)
    def _():
        c = lax.broadcasted_iota(jnp.int32, (band, BLOCK_A), 0)
        r = lax.broadcasted_iota(jnp.int32, (band, BLOCK_A), 1)
        for var in range(3):
            dist = jnp.abs(r - c + var * BLOCK_A)
            allowed = dist <= WINDOW
            distf = dist.astype(F32)
            for h in range(N_Q_HEADS):
                bias_ref[var, h] = jnp.where(allowed, -(_alibi_slope(h) * LOG2E) * distf, NEG_INF)

        mvt_ref[...] = mkv_ref[:, MW:].T

    slab_blk = _slab_block_a(i, nb)
    masks = _slot_masks(BLOCK_A)

    def store_pair(rows, t, z):
        y_ref[rows, t * LANES:(t + 1) * LANES] = z.T.astype(BF16)

    def token_unit(s, t, off, var):
        rows = slice(s * BLOCK_A, (s + 1) * BLOCK_A)
        heads = (2 * t, 2 * t + 1)
        kvh = [h // GROUP_A for h in heads]

        def k_all():
            if t == 0:
                vbt_ref[s] = kv_ref[pl.ds(off, band), KVW_A:].T
            return kv_ref[pl.ds(off, band), :KVW_A]

        def q2():
            return jnp.concatenate(
                [q_ref[rows, (h % GROUP_A) * QUAD:(h % GROUP_A + 1) * QUAD] * masks[j]
                 for h, j in zip(heads, kvh)], axis=0)

        return _pair_unit(
            k_all, q2, lambda: vbt_ref[s, kvh[0] * HEAD_DIM:(kvh[1] + 1) * HEAD_DIM, :],
            (kvh[1] - kvh[0]) * HEAD_DIM, [sink_ref[h] * LOG2E for h in heads],
            [lambda keys, h=h: bias_ref[var, h, keys, :] for h in heads],
            functools.partial(store_pair, rows, t))

    units = []
    for s in range(SUB_BLOCKS_A):
        g_blk = i * SUB_BLOCKS_A + s
        band_blk = jnp.clip(g_blk - 1, 0, nb - 3)
        off = pl.multiple_of((band_blk - slab_blk) * BLOCK_A, BLOCK_A)
        var = g_blk - band_blk
        rows = slice(s * BLOCK_A, (s + 1) * BLOCK_A)
        units += [token_unit(s, t, off, var) for t in range(N_Q_HEADS // 2)]
        units += _mem_units(qm_ref, rows, mkv_ref, mvt_ref, masks,
                            lambda t, z, rows=rows: store_pair(rows, N_Q_HEADS // 2 + t, z))
    _run_pipelined(units, STAGE_DELAY_A)


def _attn_a(proj, mkv, layer, sink, casts):
    seq, _ = proj.shape
    nb = seq // BLOCK_A
    tq = SUB_BLOCKS_A * BLOCK_A
    slab = (SUB_BLOCKS_A + 2) * BLOCK_A
    n_steps = nb // SUB_BLOCKS_A
    jobs = [_CastJob(w, l, n_steps, rg) for w, l, rg in casts]
    outs = pl.pallas_call(
        functools.partial(_attn_a_kernel, nb=nb, jobs=jobs),
        grid=(n_steps,),
        in_specs=[
            pl.BlockSpec(memory_space=pltpu.SMEM),
            pl.BlockSpec((tq, QW), lambda i: (i, 0)),
            pl.BlockSpec((tq, MW), lambda i: (i, (QW + 2 * KVW_A) // MW)),
            pl.BlockSpec((pl.Element(slab), pl.Element(2 * KVW_A)),
                         lambda i: (_slab_block_a(i, nb) * BLOCK_A, QW)),
            _layer_block(mkv, layer),
        ] + [job.in_spec for job in jobs],
        out_specs=[pl.BlockSpec((tq, QW + MW), lambda i: (i, 0))] + [job.out_spec for job in jobs],
        out_shape=[jax.ShapeDtypeStruct((seq, QW + MW), BF16)] + [job.out_shape for job in jobs],
        scratch_shapes=[pltpu.VMEM((3, N_Q_HEADS, 3 * BLOCK_A, BLOCK_A), F32),
                        pltpu.VMEM((MW, mkv.shape[1]), BF16),
                        pltpu.VMEM((SUB_BLOCKS_A, KVW_A, 3 * BLOCK_A), BF16)],
        compiler_params=pltpu.CompilerParams(
            dimension_semantics=("arbitrary",), vmem_limit_bytes=VMEM_LIMIT_BYTES),
        name="attn_a",
    )(sink, proj, proj, proj, mkv, *[w for w, _, _ in casts])
    return outs[0], outs[1:]


N_DR = 2 * NA_ROWS - 1
N_DC = 2 * NA_COLS - 1
N_QUADS = N_Q_HEADS // 4


def _bias_table_kernel(rpb_ref, o_ref):
    li = pl.program_id(0)
    hq = pl.program_id(1)
    kc = lax.broadcasted_iota(jnp.int32, (GRID_W, QUAD), 0)
    lane = lax.broadcasted_iota(jnp.int32, (GRID_W, QUAD), 1)
    qc = lane % GRID_W
    col_start = jnp.clip(qc - NA_COLS // 2, 0, GRID_W - NA_COLS)
    in_win = (kc >= col_start) & (kc < col_start + NA_COLS)
    idx = jnp.clip(kc - qc, -(NA_COLS - 1), NA_COLS - 1) + NA_COLS - 1
    slot_row = _head_slot((1, QUAD))
    base = (li * N_Q_HEADS + 4 * hq) * (N_DR * N_DC)

    def body(d, carry):
        acc = jnp.zeros((GRID_W, QUAD), F32)
        for v in range(N_DC):
            a = [rpb_ref[base + hl * (N_DR * N_DC) + d * N_DC + v] for hl in range(4)]
            val = jnp.where(slot_row == 0, a[0],
                            jnp.where(slot_row == 1, a[1], jnp.where(slot_row == 2, a[2], a[3])))
            acc = jnp.where(idx == v, val, acc)
        o_ref[pl.ds(pl.multiple_of(d * GRID_W, GRID_W), GRID_W), :] = jnp.where(in_win, acc * LOG2E, NEG_INF)
        return carry

    lax.fori_loop(0, N_DR, body, 0)


def _bias_tables(rpb_b):
    n_layers = rpb_b.shape[0]
    return pl.pallas_call(
        _bias_table_kernel,
        grid=(n_layers, N_QUADS),
        in_specs=[pl.BlockSpec(memory_space=pltpu.SMEM)],
        out_specs=pl.BlockSpec((None, None, N_DR * GRID_W, QUAD),
                               lambda li, hq: (li, hq, 0, 0)),
        out_shape=jax.ShapeDtypeStruct((n_layers, N_QUADS, N_DR * GRID_W, QUAD), F32),
        name="rpb_table",
    )(rpb_b.reshape(-1))


def _slab_start_row(b, rows):
    return jnp.clip(b * ROWS_PER_STEP_B - NA_ROWS // 2, 0, rows - 2 * ROWS_PER_STEP_B)


def _attn_b_kernel(q_ref, qm_ref, kv_ref, bias_ref, mkv_ref, *rest, rows, jobs):
    n = len(jobs)
    y_ref, mvt_ref = rest[n], rest[2 * n + 1]
    b = pl.program_id(0)
    for job, src_ref, dst_ref in zip(jobs, rest[:n], rest[n + 1:2 * n + 1]):
        _cast_chunk(b, job, src_ref, dst_ref)
    win = NA_ROWS * GRID_W
    slot = _head_slot((GRID_W, QUAD))
    slab0 = _slab_start_row(b, rows)

    @pl.when(b == 0)
    def _():
        mvt_ref[...] = mkv_ref[:, MW:].T

    masks64 = _slot_masks(GRID_W)
    masks128 = _slot_masks(LANES)

    def quad_unit(j, hq):
        r = b * ROWS_PER_STEP_B + j
        row_start = jnp.clip(r - NA_ROWS // 2, 0, rows - NA_ROWS)
        koff = pl.multiple_of((row_start - slab0) * GRID_W, GRID_W)
        boff = pl.multiple_of((row_start - r + NA_ROWS - 1) * GRID_W, GRID_W)
        qrows = slice(j * GRID_W, (j + 1) * GRID_W)
        lo = hq * QUAD

        def scores(_):
            q4 = q_ref[qrows, lo:lo + QUAD]
            qd = jnp.concatenate([q4 * masks64[hl] for hl in range(4)], axis=0)
            k4 = kv_ref[pl.ds(koff, win), lo:lo + QUAD]
            st = lax.dot_general(k4, qd, _NT, preferred_element_type=F32)
            st = st + bias_ref[hq, pl.ds(boff, win), :]
            return st, jnp.max(st, axis=0, keepdims=True)

        def softmax(state):
            st, m = state
            e = jnp.exp2(st - m)
            return e.astype(BF16), jnp.sum(e, axis=0, keepdims=True)

        def weighted_values(state):
            p, l = state
            v4 = kv_ref[pl.ds(koff, win), QW + lo:QW + lo + QUAD]
            o2 = lax.dot_general(p, v4, _TN, preferred_element_type=F32)
            inv = jnp.broadcast_to(1.0 / l, (LANES, QUAD)).T
            y4 = o2[:GRID_W] * jnp.tile(inv[:GRID_W], (1, 2))
            for hl in range(1, 4):
                part = o2[hl * GRID_W:(hl + 1) * GRID_W] * jnp.tile(inv[hl * GRID_W:(hl + 1) * GRID_W], (1, 2))
                y4 = jnp.where(slot == hl, part, y4)
            y_ref[qrows, lo:lo + QUAD] = y4.astype(BF16)

        return scores, softmax, weighted_values

    def store_mem_pair(blk, t, z):
        y_ref[blk, QW + t * LANES:QW + (t + 1) * LANES] = z.T.astype(BF16)

    units = [quad_unit(j, hq) for j in range(ROWS_PER_STEP_B) for hq in range(N_QUADS)]
    for s in range(ROWS_PER_STEP_B * GRID_W // LANES):
        blk = slice(s * LANES, (s + 1) * LANES)
        units += _mem_units(qm_ref, blk, mkv_ref, mvt_ref, masks128,
                            functools.partial(store_mem_pair, blk))
    _run_pipelined(units, STAGE_DELAY_B)


def _attn_b(proj, mkv, layer, bias, bias_layer, casts):
    seq, _ = proj.shape
    rows = seq // GRID_W
    tq = ROWS_PER_STEP_B * GRID_W
    slab = 2 * ROWS_PER_STEP_B * GRID_W
    n_steps = rows // ROWS_PER_STEP_B
    jobs = [_CastJob(w, l, n_steps, rg) for w, l, rg in casts]
    outs = pl.pallas_call(
        functools.partial(_attn_b_kernel, rows=rows, jobs=jobs),
        grid=(n_steps,),
        in_specs=[
            pl.BlockSpec((tq, QW), lambda b: (b, 0)),
            pl.BlockSpec((tq, MW), lambda b: (b, 3 * QW // MW)),
            pl.BlockSpec((pl.Element(slab), pl.Element(2 * QW)),
                         lambda b: (_slab_start_row(b, rows) * GRID_W, QW)),
            _layer_block(bias, bias_layer),
            _layer_block(mkv, layer),
        ] + [job.in_spec for job in jobs],
        out_specs=[pl.BlockSpec((tq, QW + MW), lambda b: (b, 0))] + [job.out_spec for job in jobs],
        out_shape=[jax.ShapeDtypeStruct((seq, QW + MW), BF16)] + [job.out_shape for job in jobs],
        scratch_shapes=[pltpu.VMEM((MW, mkv.shape[1]), BF16)],
        compiler_params=pltpu.CompilerParams(
            dimension_semantics=("arbitrary",), vmem_limit_bytes=VMEM_LIMIT_BYTES),
        name="attn_b",
    )(proj, proj, proj, bias, mkv, *[w for w, _, _ in casts])
    return outs[0], outs[1:]


def _post_kernel(x_ref, y_ref, wo_ref, gf_ref, wgu_ref, wd_ref, gn_ref, *rest, d_ff, last):
    tiles = [slice(k * SUB_TILE, (k + 1) * SUB_TILE) for k in range(ROW_TILE // SUB_TILE)]
    x1 = [x_ref[rows, :] + jnp.dot(y_ref[rows, :], wo_ref[...], preferred_element_type=F32)
          for rows in tiles]
    act = []
    for k in range(len(tiles)):
        h = _rms(x1[k], gf_ref[...]).astype(BF16)
        gu = jnp.dot(h, wgu_ref[...], preferred_element_type=F32)
        g = gu[:, :d_ff]
        u = gu[:, d_ff:]
        act.append((g * (1.0 / (1.0 + jnp.exp(-g))) * u).astype(BF16))
    x2 = [x1[k] + jnp.dot(act[k], wd_ref[...], preferred_element_type=F32)
          for k in range(len(tiles))]
    if last:
        (o_ref,) = rest
        for k, rows in enumerate(tiles):
            o_ref[rows, :] = _rms(x2[k], gn_ref[...])
    else:
        wn_ref, cs_ref, o_ref, p_ref = rest
        for k, rows in enumerate(tiles):
            o_ref[rows, :] = x2[k]
            hn = _rms(x2[k], gn_ref[...]).astype(BF16)
            p_ref[rows, :] = (jnp.dot(hn, wn_ref[...], preferred_element_type=F32)
                              * cs_ref[...]).astype(BF16)


def _post(x, y, w_out, g_ffn, layer, w_gu, w_d, g_next, next_layer, w_next):
    s, d = x.shape
    d_ff = w_d.shape[0]
    last = w_next is None
    row = lambda i: (i, 0)
    in_specs = [pl.BlockSpec((ROW_TILE, d), row), pl.BlockSpec((ROW_TILE, y.shape[1]), row),
                _resident(w_out.shape), _layer_block(g_ffn, layer), _resident(w_gu.shape),
                _resident(w_d.shape), _layer_block(g_next, next_layer)]
    args = [x, y, w_out, g_ffn, w_gu, w_d, g_next]
    out_specs = [pl.BlockSpec((ROW_TILE, d), row)]
    out_shape = [jax.ShapeDtypeStruct((s, d), F32)]
    if not last:
        n = w_next.shape[1]
        in_specs += [_resident(w_next.shape), _resident((1, n))]
        args += [w_next, _query_col_scale(n, QW, MW)]
        out_specs.append(pl.BlockSpec((ROW_TILE, n), row))
        out_shape.append(jax.ShapeDtypeStruct((s, n), BF16))
    outs = pl.pallas_call(
        functools.partial(_post_kernel, d_ff=d_ff, last=last),
        grid=(s // ROW_TILE,),
        in_specs=in_specs,
        out_specs=out_specs,
        out_shape=out_shape,
        compiler_params=pltpu.CompilerParams(
            dimension_semantics=("arbitrary",), vmem_limit_bytes=VMEM_LIMIT_BYTES),
        name="post_last" if last else "post",
    )(*args)
    return (outs[0], None) if last else (outs[0], outs[1])


def kernel(x, mem, norm_mix, norm_ffn, norm_mem, norm_final, w_in_a, sink_a, w_in_b, rpb_b,
           w_mem_kv, w_out, w_gate_up, w_down):
    batch, seq, d = x.shape
    assert batch == 1 and mem.shape[0] == 1
    assert seq % (ROWS_PER_STEP_B * GRID_W) == 0 and seq % ROW_TILE == 0 and seq % PROJ_TILE == 0
    assert seq % (SUB_BLOCKS_A * BLOCK_A) == 0 and seq // BLOCK_A >= SUB_BLOCKS_A + 2
    depth = w_out.shape[0]
    xs = x.reshape(seq, d)
    w_in = (w_in_a, w_in_b)
    regroup = (_q_regroup_moves_a(w_in_a.shape[2]), None)
    g_mix = norm_mix.reshape(depth, 1, d)
    g_ffn = norm_ffn.reshape(depth, 1, d)
    g_final = norm_final.reshape(1, 1, d)

    mkv, w_in0 = _mem_kv(mem.reshape(mem.shape[1], d), norm_mem, w_mem_kv, w_in[0], regroup[0])
    bias = _bias_tables(rpb_b)

    proj = _proj(xs, g_mix, 0, w_in0)
    for i in range(depth):
        last = i == depth - 1
        nxt = (i + 1) % 2
        casts = [(w_out, i, None), (w_gate_up, i, None), (w_down, i, None)]
        if not last:
            casts.append((w_in[nxt], (i + 1) // 2, regroup[nxt]))
        if i % 2 == 0:
            y, wb = _attn_a(proj, mkv, i, sink_a[i // 2], casts)
        else:
            y, wb = _attn_b(proj, mkv, i, bias, i // 2, casts)
        if last:
            xs, proj = _post(xs, y, wb[0], g_ffn, i, wb[1], wb[2], g_final, 0, None)
        else:
            xs, proj = _post(xs, y, wb[0], g_ffn, i, wb[1], wb[2], g_mix, i + 1, wb[3])
    return xs.reshape(batch, seq, d)
```

```python
import functools

import numpy as np

import jax
import jax.numpy as jnp
from jax import lax
from jax.experimental import pallas as pl
from jax.experimental.pallas import tpu as pltpu

F32 = jnp.float32
BF16 = jnp.bfloat16

HEAD_DIM = 64
N_Q_HEADS = 12
N_KV_HEADS_A = 4
GROUP_A = N_Q_HEADS // N_KV_HEADS_A
N_MEM_HEADS = 4
WINDOW = 128
BLOCK_A = 128
SUB_BLOCKS_A = 4
GRID_W = 64
NA_ROWS = 8
NA_COLS = 16
ROWS_PER_STEP_B = 8
EPS = 1e-6
NEG_INF = -1e30
LOG2E = 1.4426950408889634
Q_SCALE = HEAD_DIM ** -0.5 * LOG2E

LANES = 128
BF16_ROWS = 16
QW = N_Q_HEADS * HEAD_DIM
KVW_A = N_KV_HEADS_A * HEAD_DIM
MW = N_MEM_HEADS * HEAD_DIM
QUAD = 4 * HEAD_DIM

VMEM_LIMIT_BYTES = 56 * 1024 * 1024
ROW_TILE = 512
LAST_ROW_TILE = 1024
PROJ_TILE = 2048
SUB_TILE = 256

_NT = (((1,), (1,)), ((), ()))
_TN = (((0,), (0,)), ((), ()))


def _resident(shape):
    n = len(shape)
    return pl.BlockSpec(shape, lambda *_: (0,) * n)


def _layer_block(arr, layer):
    n = arr.ndim - 1
    return pl.BlockSpec((None,) + arr.shape[1:], lambda *_: (layer,) + (0,) * n)


class _CastJob:
    def __init__(self, w, layer, n_steps, regroup=None):
        _, n_rows, n_cols = w.shape
        self.regroup = regroup
        self.n_chunks = n_steps
        while n_rows % (self.n_chunks * BF16_ROWS):
            self.n_chunks //= 2
        rc = n_rows // self.n_chunks
        last = self.n_chunks - 1
        self.in_spec = pl.BlockSpec((None, rc, n_cols), lambda s: (layer, jnp.minimum(s, last), 0))
        self.out_spec = pl.BlockSpec((rc, n_cols), lambda s: (jnp.minimum(s, last), 0))
        self.out_shape = jax.ShapeDtypeStruct((n_rows, n_cols), BF16)


def _cast_chunk(step, job, src_ref, dst_ref):
    @pl.when(step < job.n_chunks)
    def _():
        blk = src_ref[...]
        if job.regroup is None:
            dst_ref[...] = blk.astype(BF16)
        else:
            for dst, srcs, width in job.regroup:
                parts = [blk[:, src:src + width] for src in srcs]
                val = parts[0] if len(parts) == 1 else jnp.concatenate(parts, axis=1)
                dst_ref[:, dst:dst + len(srcs) * width] = val.astype(BF16)


def _rms(x, g):
    return x * lax.rsqrt(jnp.mean(x * x, axis=-1, keepdims=True) + EPS) * g


def _head_slot(shape):
    return lax.broadcasted_iota(jnp.int32, shape, 1) // HEAD_DIM


def _slot_masks(rows):
    slot = _head_slot((rows, QUAD))
    return [(slot == j).astype(BF16) for j in range(QUAD // HEAD_DIM)]


STAGE_DELAY_A = (0, 2)
STAGE_DELAY_B = (0, 1, 1)


def _run_pipelined(units, delays):
    state = {}
    for t in range(len(units) + delays[-1]):
        for k, delay in enumerate(delays):
            u = t - delay
            if 0 <= u < len(units):
                stages = units[u]
                if len(stages) == len(delays):
                    state[u] = stages[k](state.get(u))
                elif k in (0, len(delays) - 1):
                    state[u] = stages[min(k, 1)](state.get(u))


def _pair_unit(k_all, q2, vt2, second_row, sinks, biases, store):
    def scores(_):
        return lax.dot_general(k_all(), q2(), _NT, preferred_element_type=F32)

    def weighted_values(st):
        v = vt2()
        n_v = v.shape[0]
        lhs = jnp.concatenate([v, jnp.ones((BF16_ROWS, v.shape[1]), BF16)], axis=0)
        sink_row = None
        if sinks[0] is not None:
            sink_row = jnp.concatenate([jnp.full((1, LANES), s_, F32) for s_ in sinks], axis=1)
        m = sink_row if sink_row is not None else jnp.full((1, 2 * LANES), NEG_INF, F32)
        acc = None
        for c in range(st.shape[0] // LANES):
            keys = slice(c * LANES, (c + 1) * LANES)
            s_c = st[keys, :]
            if biases[0] is not None:
                s_c = s_c + jnp.concatenate([b(keys) for b in biases], axis=1)
            m_new = jnp.maximum(m, jnp.max(s_c, axis=0, keepdims=True))
            e = jnp.exp2(s_c - m_new).astype(BF16)
            part = jnp.dot(lhs[:, keys], e, preferred_element_type=F32)
            acc = part if acc is None else acc * jnp.exp2(m - m_new) + part
            m = m_new
        l = acc[n_v:n_v + 1, :]
        if sink_row is not None:
            l = l + jnp.exp2(sink_row - m)
        inv = 1.0 / l
        outs = []
        for p in range(2):
            cols = slice(p * LANES, (p + 1) * LANES)
            r0 = p * second_row
            outs.append(acc[r0:r0 + HEAD_DIM, cols] * inv[:, cols])
        store(jnp.concatenate(outs, axis=0))

    return scores, weighted_values


def _mem_units(qm_ref, rows, mkv_ref, mvt_ref, masks, store_pair):
    def q2(t):
        qm = qm_ref[rows, :]
        return jnp.concatenate([qm * masks[2 * t], qm * masks[2 * t + 1]], axis=0)

    return [_pair_unit(lambda: mkv_ref[:, :MW], functools.partial(q2, t),
                       lambda t=t: mvt_ref[t * LANES:(t + 1) * LANES, :], HEAD_DIM,
                       (None, None), (None, None), functools.partial(store_pair, t))
            for t in range(N_MEM_HEADS // 2)]


def _mem_kv_kernel(mem_ref, g_ref, w_ref, w0_ref, o_ref, w0_out_ref, *, job):
    mem_n = _rms(mem_ref[...], g_ref[...])
    o_ref[...] = jnp.dot(mem_n.astype(BF16), w_ref[...].astype(BF16),
                         preferred_element_type=F32).astype(BF16)
    _cast_chunk(pl.program_id(0), job, w0_ref, w0_out_ref)


def _mem_kv(mem, norm_mem, w_mem_kv, w_in0, regroup0):
    depth, d, w = w_mem_kv.shape
    n_mem = mem.shape[0]
    job = _CastJob(w_in0, 0, depth, regroup0)
    return pl.pallas_call(
        functools.partial(_mem_kv_kernel, job=job),
        grid=(depth,),
        in_specs=[_resident((n_mem, d)), _resident((1, d)),
                  pl.BlockSpec((None, d, w), lambda i: (i, 0, 0)), job.in_spec],
        out_specs=[pl.BlockSpec((None, n_mem, w), lambda i: (i, 0, 0)), job.out_spec],
        out_shape=[jax.ShapeDtypeStruct((depth, n_mem, w), BF16), job.out_shape],
        name="mem_kv",
    )(mem, norm_mem.reshape(1, d), w_mem_kv, w_in0)


def _query_col_scale(width, n_q, n_qm):
    cs = np.ones((1, width), np.float32)
    cs[:, :n_q] = Q_SCALE
    cs[:, width - n_qm:] = Q_SCALE
    return jnp.asarray(cs)


def _proj_kernel(x_ref, g_ref, w_ref, cs_ref, o_ref):
    tiles = [slice(k * SUB_TILE, (k + 1) * SUB_TILE) for k in range(PROJ_TILE // SUB_TILE)]
    h = _rms(x_ref[tiles[0], :], g_ref[...]).astype(BF16)
    for k, rows in enumerate(tiles):
        h_next = None
        if k + 1 < len(tiles):
            h_next = _rms(x_ref[tiles[k + 1], :], g_ref[...]).astype(BF16)
        o_ref[rows, :] = (jnp.dot(h, w_ref[...], preferred_element_type=F32)
                          * cs_ref[...]).astype(BF16)
        h = h_next


def _proj(x, g, layer, w):
    s, d = x.shape
    n = w.shape[1]
    return pl.pallas_call(
        _proj_kernel,
        grid=(s // PROJ_TILE,),
        in_specs=[pl.BlockSpec((PROJ_TILE, d), lambda i: (i, 0)),
                  _layer_block(g, layer), _resident(w.shape), _resident((1, n))],
        out_specs=pl.BlockSpec((PROJ_TILE, n), lambda i: (i, 0)),
        out_shape=jax.ShapeDtypeStruct((s, n), BF16),
        compiler_params=pltpu.CompilerParams(
            dimension_semantics=("arbitrary",), vmem_limit_bytes=VMEM_LIMIT_BYTES),
        name="in_proj",
    )(x, g, w, _query_col_scale(n, QW, MW))


def _alibi_slope(h):
    return 2.0 ** (-8.0 * (h + 1) / N_Q_HEADS)


def _q_regroup_moves_a(width):
    heads = [GROUP_A * j + g for g in range(GROUP_A) for j in range(N_KV_HEADS_A)]
    moves = [(t * LANES, [heads[2 * t] * HEAD_DIM, heads[2 * t + 1] * HEAD_DIM], HEAD_DIM)
             for t in range(N_Q_HEADS // 2)]
    return moves + [(QW, [QW], width - QW)]


def _slab_block_a(i, nb):
    return jnp.clip(i * SUB_BLOCKS_A - 1, 0, nb - (SUB_BLOCKS_A + 2))


def _attn_a_kernel(sink_ref, q_ref, qm_ref, kv_ref, mkv_ref, *rest, nb, jobs):
    n = len(jobs)
    y_ref, bias_ref, mvt_ref, vbt_ref = rest[n], rest[2 * n + 1], rest[2 * n + 2], rest[2 * n + 3]
    i = pl.program_id(0)
    band = 3 * BLOCK_A
    for job, src_ref, dst_ref in zip(jobs, rest[:n], rest[n + 1:2 * n + 1]):
        _cast_chunk(i, job, src_ref, dst_ref)

    @pl.when(i == 0)
    def _():
        c = lax.broadcasted_iota(jnp.int32, (band, BLOCK_A), 0)
        r = lax.broadcasted_iota(jnp.int32, (band, BLOCK_A), 1)
        for var in range(3):
            dist = jnp.abs(r - c + var * BLOCK_A)
            allowed = dist <= WINDOW
            distf = dist.astype(F32)
            for h in range(N_Q_HEADS):
                bias_ref[var, h] = jnp.where(allowed, -(_alibi_slope(h) * LOG2E) * distf, NEG_INF)

        mvt_ref[...] = mkv_ref[:, MW:].T

    slab_blk = _slab_block_a(i, nb)
    masks = _slot_masks(BLOCK_A)

    def store_pair(rows, t, z):
        y_ref[rows, t * LANES:(t + 1) * LANES] = z.T.astype(BF16)

    def token_unit(s, t, off, var):
        rows = slice(s * BLOCK_A, (s + 1) * BLOCK_A)
        heads = (2 * t, 2 * t + 1)
        kvh = [h // GROUP_A for h in heads]

        def k_all():
            if t == 0:
                vbt_ref[s] = kv_ref[pl.ds(off, band), KVW_A:].T
            return kv_ref[pl.ds(off, band), :KVW_A]

        def q2():
            return jnp.concatenate(
                [q_ref[rows, (h % GROUP_A) * QUAD:(h % GROUP_A + 1) * QUAD] * masks[j]
                 for h, j in zip(heads, kvh)], axis=0)

        return _pair_unit(
            k_all, q2, lambda: vbt_ref[s, kvh[0] * HEAD_DIM:(kvh[1] + 1) * HEAD_DIM, :],
            (kvh[1] - kvh[0]) * HEAD_DIM, [sink_ref[h] * LOG2E for h in heads],
            [lambda keys, h=h: bias_ref[var, h, keys, :] for h in heads],
            functools.partial(store_pair, rows, t))

    units = []
    for s in range(SUB_BLOCKS_A):
        g_blk = i * SUB_BLOCKS_A + s
        band_blk = jnp.clip(g_blk - 1, 0, nb - 3)
        off = pl.multiple_of((band_blk - slab_blk) * BLOCK_A, BLOCK_A)
        var = g_blk - band_blk
        rows = slice(s * BLOCK_A, (s + 1) * BLOCK_A)
        units += [token_unit(s, t, off, var) for t in range(N_Q_HEADS // 2)]
        units += _mem_units(qm_ref, rows, mkv_ref, mvt_ref, masks,
                            lambda t, z, rows=rows: store_pair(rows, N_Q_HEADS // 2 + t, z))
    _run_pipelined(units, STAGE_DELAY_A)


def _attn_a(proj, mkv, layer, sink, casts):
    seq, _ = proj.shape
    nb = seq // BLOCK_A
    tq = SUB_BLOCKS_A * BLOCK_A
    slab = (SUB_BLOCKS_A + 2) * BLOCK_A
    n_steps = nb // SUB_BLOCKS_A
    jobs = [_CastJob(w, l, n_steps, rg) for w, l, rg in casts]
    outs = pl.pallas_call(
        functools.partial(_attn_a_kernel, nb=nb, jobs=jobs),
        grid=(n_steps,),
        in_specs=[
            pl.BlockSpec(memory_space=pltpu.SMEM),
            pl.BlockSpec((tq, QW), lambda i: (i, 0)),
            pl.BlockSpec((tq, MW), lambda i: (i, (QW + 2 * KVW_A) // MW)),
            pl.BlockSpec((pl.Element(slab), pl.Element(2 * KVW_A)),
                         lambda i: (_slab_block_a(i, nb) * BLOCK_A, QW)),
            _layer_block(mkv, layer),
        ] + [job.in_spec for job in jobs],
        out_specs=[pl.BlockSpec((tq, QW + MW), lambda i: (i, 0))] + [job.out_spec for job in jobs],
        out_shape=[jax.ShapeDtypeStruct((seq, QW + MW), BF16)] + [job.out_shape for job in jobs],
        scratch_shapes=[pltpu.VMEM((3, N_Q_HEADS, 3 * BLOCK_A, BLOCK_A), F32),
                        pltpu.VMEM((MW, mkv.shape[1]), BF16),
                        pltpu.VMEM((SUB_BLOCKS_A, KVW_A, 3 * BLOCK_A), BF16)],
        compiler_params=pltpu.CompilerParams(
            dimension_semantics=("arbitrary",), vmem_limit_bytes=VMEM_LIMIT_BYTES),
        name="attn_a",
    )(sink, proj, proj, proj, mkv, *[w for w, _, _ in casts])
    return outs[0], outs[1:]


N_DR = 2 * NA_ROWS - 1
N_DC = 2 * NA_COLS - 1
N_QUADS = N_Q_HEADS // 4


def _bias_table_kernel(rpb_ref, o_ref):
    li = pl.program_id(0)
    hq = pl.program_id(1)
    kc = lax.broadcasted_iota(jnp.int32, (GRID_W, QUAD), 0)
    lane = lax.broadcasted_iota(jnp.int32, (GRID_W, QUAD), 1)
    qc = lane % GRID_W
    col_start = jnp.clip(qc - NA_COLS // 2, 0, GRID_W - NA_COLS)
    in_win = (kc >= col_start) & (kc < col_start + NA_COLS)
    idx = jnp.clip(kc - qc, -(NA_COLS - 1), NA_COLS - 1) + NA_COLS - 1
    slot_row = _head_slot((1, QUAD))
    base = (li * N_Q_HEADS + 4 * hq) * (N_DR * N_DC)

    def body(d, carry):
        acc = jnp.zeros((GRID_W, QUAD), F32)
        for v in range(N_DC):
            a = [rpb_ref[base + hl * (N_DR * N_DC) + d * N_DC + v] for hl in range(4)]
            val = jnp.where(slot_row == 0, a[0],
                            jnp.where(slot_row == 1, a[1], jnp.where(slot_row == 2, a[2], a[3])))
            acc = jnp.where(idx == v, val, acc)
        o_ref[pl.ds(pl.multiple_of(d * GRID_W, GRID_W), GRID_W), :] = jnp.where(in_win, acc * LOG2E, NEG_INF)
        return carry

    lax.fori_loop(0, N_DR, body, 0)


def _bias_tables(rpb_b):
    n_layers = rpb_b.shape[0]
    return pl.pallas_call(
        _bias_table_kernel,
        grid=(n_layers, N_QUADS),
        in_specs=[pl.BlockSpec(memory_space=pltpu.SMEM)],
        out_specs=pl.BlockSpec((None, None, N_DR * GRID_W, QUAD),
                               lambda li, hq: (li, hq, 0, 0)),
        out_shape=jax.ShapeDtypeStruct((n_layers, N_QUADS, N_DR * GRID_W, QUAD), F32),
        name="rpb_table",
    )(rpb_b.reshape(-1))


def _slab_start_row(b, rows):
    return jnp.clip(b * ROWS_PER_STEP_B - NA_ROWS // 2, 0, rows - 2 * ROWS_PER_STEP_B)


def _attn_b_kernel(q_ref, qm_ref, kv_ref, bias_ref, mkv_ref, *rest, rows, jobs):
    n = len(jobs)
    y_ref, mvt_ref = rest[n], rest[2 * n + 1]
    b = pl.program_id(0)
    for job, src_ref, dst_ref in zip(jobs, rest[:n], rest[n + 1:2 * n + 1]):
        _cast_chunk(b, job, src_ref, dst_ref)
    win = NA_ROWS * GRID_W
    slot = _head_slot((GRID_W, QUAD))
    slab0 = _slab_start_row(b, rows)

    @pl.when(b == 0)
    def _():
        mvt_ref[...] = mkv_ref[:, MW:].T

    masks64 = _slot_masks(GRID_W)
    masks128 = _slot_masks(LANES)

    def quad_unit(j, hq):
        r = b * ROWS_PER_STEP_B + j
        row_start = jnp.clip(r - NA_ROWS // 2, 0, rows - NA_ROWS)
        koff = pl.multiple_of((row_start - slab0) * GRID_W, GRID_W)
        boff = pl.multiple_of((row_start - r + NA_ROWS - 1) * GRID_W, GRID_W)
        qrows = slice(j * GRID_W, (j + 1) * GRID_W)
        lo = hq * QUAD

        def scores(_):
            q4 = q_ref[qrows, lo:lo + QUAD]
            qd = jnp.concatenate([q4 * masks64[hl] for hl in range(4)], axis=0)
            k4 = kv_ref[pl.ds(koff, win), lo:lo + QUAD]
            st = lax.dot_general(k4, qd, _NT, preferred_element_type=F32)
            st = st + bias_ref[hq, pl.ds(boff, win), :]
            return st, jnp.max(st, axis=0, keepdims=True)

        def softmax(state):
            st, m = state
            e = jnp.exp2(st - m)
            return e.astype(BF16), jnp.sum(e, axis=0, keepdims=True)

        def weighted_values(state):
            p, l = state
            v4 = kv_ref[pl.ds(koff, win), QW + lo:QW + lo + QUAD]
            o2 = lax.dot_general(p, v4, _TN, preferred_element_type=F32)
            inv = jnp.broadcast_to(1.0 / l, (LANES, QUAD)).T
            y4 = o2[:GRID_W] * jnp.tile(inv[:GRID_W], (1, 2))
            for hl in range(1, 4):
                part = o2[hl * GRID_W:(hl + 1) * GRID_W] * jnp.tile(inv[hl * GRID_W:(hl + 1) * GRID_W], (1, 2))
                y4 = jnp.where(slot == hl, part, y4)
            y_ref[qrows, lo:lo + QUAD] = y4.astype(BF16)

        return scores, softmax, weighted_values

    def store_mem_pair(blk, t, z):
        y_ref[blk, QW + t * LANES:QW + (t + 1) * LANES] = z.T.astype(BF16)

    units = [quad_unit(j, hq) for j in range(ROWS_PER_STEP_B) for hq in range(N_QUADS)]
    for s in range(ROWS_PER_STEP_B * GRID_W // LANES):
        blk = slice(s * LANES, (s + 1) * LANES)
        units += _mem_units(qm_ref, blk, mkv_ref, mvt_ref, masks128,
                            functools.partial(store_mem_pair, blk))
    _run_pipelined(units, STAGE_DELAY_B)


def _attn_b(proj, mkv, layer, bias, bias_layer, casts):
    seq, _ = proj.shape
    rows = seq // GRID_W
    tq = ROWS_PER_STEP_B * GRID_W
    slab = 2 * ROWS_PER_STEP_B * GRID_W
    n_steps = rows // ROWS_PER_STEP_B
    jobs = [_CastJob(w, l, n_steps, rg) for w, l, rg in casts]
    outs = pl.pallas_call(
        functools.partial(_attn_b_kernel, rows=rows, jobs=jobs),
        grid=(n_steps,),
        in_specs=[
            pl.BlockSpec((tq, QW), lambda b: (b, 0)),
            pl.BlockSpec((tq, MW), lambda b: (b, 3 * QW // MW)),
            pl.BlockSpec((pl.Element(slab), pl.Element(2 * QW)),
                         lambda b: (_slab_start_row(b, rows) * GRID_W, QW)),
            _layer_block(bias, bias_layer),
            _layer_block(mkv, layer),
        ] + [job.in_spec for job in jobs],
        out_specs=[pl.BlockSpec((tq, QW + MW), lambda b: (b, 0))] + [job.out_spec for job in jobs],
        out_shape=[jax.ShapeDtypeStruct((seq, QW + MW), BF16)] + [job.out_shape for job in jobs],
        scratch_shapes=[pltpu.VMEM((MW, mkv.shape[1]), BF16)],
        compiler_params=pltpu.CompilerParams(
            dimension_semantics=("arbitrary",), vmem_limit_bytes=VMEM_LIMIT_BYTES),
        name="attn_b",
    )(proj, proj, proj, bias, mkv, *[w for w, _, _ in casts])
    return outs[0], outs[1:]


def _post_kernel(x_ref, y_ref, wo_ref, gf_ref, wgu_ref, wd_ref, gn_ref, *rest, d_ff, last):
    tiles = [slice(k * SUB_TILE, (k + 1) * SUB_TILE) for k in range(x_ref.shape[0] // SUB_TILE)]
    x1 = [x_ref[rows, :] + jnp.dot(y_ref[rows, :], wo_ref[...], preferred_element_type=F32)
          for rows in tiles]
    act = []
    for k in range(len(tiles)):
        h = _rms(x1[k], gf_ref[...]).astype(BF16)
        gu = jnp.dot(h, wgu_ref[...], preferred_element_type=F32)
        g = gu[:, :d_ff]
        u = gu[:, d_ff:]
        act.append((g * (1.0 / (1.0 + jnp.exp(-g))) * u).astype(BF16))
    x2 = [x1[k] + jnp.dot(act[k], wd_ref[...], preferred_element_type=F32)
          for k in range(len(tiles))]
    if last:
        (o_ref,) = rest
        for k, rows in enumerate(tiles):
            o_ref[rows, :] = _rms(x2[k], gn_ref[...])
    else:
        wn_ref, cs_ref, o_ref, p_ref = rest
        for k, rows in enumerate(tiles):
            o_ref[rows, :] = x2[k]
            hn = _rms(x2[k], gn_ref[...]).astype(BF16)
            p_ref[rows, :] = (jnp.dot(hn, wn_ref[...], preferred_element_type=F32)
                              * cs_ref[...]).astype(BF16)


def _post(x, y, w_out, g_ffn, layer, w_gu, w_d, g_next, next_layer, w_next):
    s, d = x.shape
    d_ff = w_d.shape[0]
    last = w_next is None
    row_tile = LAST_ROW_TILE if last else ROW_TILE
    row = lambda i: (i, 0)
    in_specs = [pl.BlockSpec((row_tile, d), row), pl.BlockSpec((row_tile, y.shape[1]), row),
                _resident(w_out.shape), _layer_block(g_ffn, layer), _resident(w_gu.shape),
                _resident(w_d.shape), _layer_block(g_next, next_layer)]
    args = [x, y, w_out, g_ffn, w_gu, w_d, g_next]
    out_specs = [pl.BlockSpec((row_tile, d), row)]
    out_shape = [jax.ShapeDtypeStruct((s, d), F32)]
    if not last:
        n = w_next.shape[1]
        in_specs += [_resident(w_next.shape), _resident((1, n))]
        args += [w_next, _query_col_scale(n, QW, MW)]
        out_specs.append(pl.BlockSpec((row_tile, n), row))
        out_shape.append(jax.ShapeDtypeStruct((s, n), BF16))
    outs = pl.pallas_call(
        functools.partial(_post_kernel, d_ff=d_ff, last=last),
        grid=(s // row_tile,),
        in_specs=in_specs,
        out_specs=out_specs,
        out_shape=out_shape,
        compiler_params=pltpu.CompilerParams(
            dimension_semantics=("arbitrary",), vmem_limit_bytes=VMEM_LIMIT_BYTES),
        name="post_last" if last else "post",
    )(*args)
    return (outs[0], None) if last else (outs[0], outs[1])


def kernel(x, mem, norm_mix, norm_ffn, norm_mem, norm_final, w_in_a, sink_a, w_in_b, rpb_b,
           w_mem_kv, w_out, w_gate_up, w_down):
    batch, seq, d = x.shape
    assert batch == 1 and mem.shape[0] == 1
    assert seq % (ROWS_PER_STEP_B * GRID_W) == 0 and seq % ROW_TILE == 0 and seq % PROJ_TILE == 0 and seq % LAST_ROW_TILE == 0
    assert seq % (SUB_BLOCKS_A * BLOCK_A) == 0 and seq // BLOCK_A >= SUB_BLOCKS_A + 2
    depth = w_out.shape[0]
    xs = x.reshape(seq, d)
    w_in = (w_in_a, w_in_b)
    regroup = (_q_regroup_moves_a(w_in_a.shape[2]), None)
    g_mix = norm_mix.reshape(depth, 1, d)
    g_ffn = norm_ffn.reshape(depth, 1, d)
    g_final = norm_final.reshape(1, 1, d)

    mkv, w_in0 = _mem_kv(mem.reshape(mem.shape[1], d), norm_mem, w_mem_kv, w_in[0], regroup[0])
    bias = _bias_tables(rpb_b)

    proj = _proj(xs, g_mix, 0, w_in0)
    for i in range(depth):
        last = i == depth - 1
        nxt = (i + 1) % 2
        casts = [(w_out, i, None), (w_gate_up, i, None), (w_down, i, None)]
        if not last:
            casts.append((w_in[nxt], (i + 1) // 2, regroup[nxt]))
        if i % 2 == 0:
            y, wb = _attn_a(proj, mkv, i, sink_a[i // 2], casts)
        else:
            y, wb = _attn_b(proj, mkv, i, bias, i // 2, casts)
        if last:
            xs, proj = _post(xs, y, wb[0], g_ffn, i, wb[1], wb[2], g_final, 0, None)
        else:
            xs, proj = _post(xs, y, wb[0], g_ffn, i, wb[1], wb[2], g_mix, i + 1, wb[3])
    return xs.reshape(batch, seq, d)
```

```python
import functools

import numpy as np

import jax
import jax.numpy as jnp
from jax import lax
from jax.experimental import pallas as pl
from jax.experimental.pallas import tpu as pltpu

F32 = jnp.float32
BF16 = jnp.bfloat16

HEAD_DIM = 64
N_Q_HEADS = 12
N_KV_HEADS_A = 4
GROUP_A = N_Q_HEADS // N_KV_HEADS_A
N_MEM_HEADS = 4
WINDOW = 128
BLOCK_A = 128
SUB_BLOCKS_A = 8
GRID_W = 64
NA_ROWS = 8
NA_COLS = 16
ROWS_PER_STEP_B = 8
EPS = 1e-6
NEG_INF = -1e30
LOG2E = 1.4426950408889634
Q_SCALE = HEAD_DIM ** -0.5 * LOG2E

LANES = 128
BF16_ROWS = 16
QW = N_Q_HEADS * HEAD_DIM
KVW_A = N_KV_HEADS_A * HEAD_DIM
MW = N_MEM_HEADS * HEAD_DIM
QUAD = 4 * HEAD_DIM

VMEM_LIMIT_BYTES = 56 * 1024 * 1024
ROW_TILE = 512
LAST_ROW_TILE = 1024
PROJ_TILE = 2048
SUB_TILE = 256

_NT = (((1,), (1,)), ((), ()))
_TN = (((0,), (0,)), ((), ()))


def _resident(shape):
    n = len(shape)
    return pl.BlockSpec(shape, lambda *_: (0,) * n)


def _layer_block(arr, layer):
    n = arr.ndim - 1
    return pl.BlockSpec((None,) + arr.shape[1:], lambda *_: (layer,) + (0,) * n)


class _CastJob:
    def __init__(self, w, layer, n_steps, regroup=None):
        _, n_rows, n_cols = w.shape
        self.regroup = regroup
        self.n_chunks = n_steps
        while n_rows % (self.n_chunks * BF16_ROWS):
            self.n_chunks //= 2
        rc = n_rows // self.n_chunks
        last = self.n_chunks - 1
        self.in_spec = pl.BlockSpec((None, rc, n_cols), lambda s: (layer, jnp.minimum(s, last), 0))
        self.out_spec = pl.BlockSpec((rc, n_cols), lambda s: (jnp.minimum(s, last), 0))
        self.out_shape = jax.ShapeDtypeStruct((n_rows, n_cols), BF16)


def _cast_chunk(step, job, src_ref, dst_ref):
    @pl.when(step < job.n_chunks)
    def _():
        blk = src_ref[...]
        if job.regroup is None:
            dst_ref[...] = blk.astype(BF16)
        else:
            for dst, srcs, width in job.regroup:
                parts = [blk[:, src:src + width] for src in srcs]
                val = parts[0] if len(parts) == 1 else jnp.concatenate(parts, axis=1)
                dst_ref[:, dst:dst + len(srcs) * width] = val.astype(BF16)


def _rms(x, g):
    return x * lax.rsqrt(jnp.mean(x * x, axis=-1, keepdims=True) + EPS) * g


def _head_slot(shape):
    return lax.broadcasted_iota(jnp.int32, shape, 1) // HEAD_DIM


def _slot_masks(rows):
    slot = _head_slot((rows, QUAD))
    return [(slot == j).astype(BF16) for j in range(QUAD // HEAD_DIM)]


STAGE_DELAY_A = (0, 2)
STAGE_DELAY_B = (0, 1, 1)


def _run_pipelined(units, delays):
    state = {}
    for t in range(len(units) + delays[-1]):
        for k, delay in enumerate(delays):
            u = t - delay
            if 0 <= u < len(units):
                stages = units[u]
                if len(stages) == len(delays):
                    state[u] = stages[k](state.get(u))
                elif k in (0, len(delays) - 1):
                    state[u] = stages[min(k, 1)](state.get(u))


def _pair_unit(k_all, q2, vt2, second_row, sinks, biases, store):
    def scores(_):
        return lax.dot_general(k_all(), q2(), _NT, preferred_element_type=F32)

    def weighted_values(st):
        v = vt2()
        n_v = v.shape[0]
        lhs = jnp.concatenate([v, jnp.ones((BF16_ROWS, v.shape[1]), BF16)], axis=0)
        sink_row = None
        if sinks[0] is not None:
            sink_row = jnp.concatenate([jnp.full((1, LANES), s_, F32) for s_ in sinks], axis=1)
        m = sink_row if sink_row is not None else jnp.full((1, 2 * LANES), NEG_INF, F32)
        acc = None
        for c in range(st.shape[0] // LANES):
            keys = slice(c * LANES, (c + 1) * LANES)
            s_c = st[keys, :]
            if biases[0] is not None:
                s_c = s_c + jnp.concatenate([b(keys) for b in biases], axis=1)
            m_new = jnp.maximum(m, jnp.max(s_c, axis=0, keepdims=True))
            e = jnp.exp2(s_c - m_new).astype(BF16)
            part = jnp.dot(lhs[:, keys], e, preferred_element_type=F32)
            acc = part if acc is None else acc * jnp.exp2(m - m_new) + part
            m = m_new
        l = acc[n_v:n_v + 1, :]
        if sink_row is not None:
            l = l + jnp.exp2(sink_row - m)
        inv = 1.0 / l
        outs = []
        for p in range(2):
            cols = slice(p * LANES, (p + 1) * LANES)
            r0 = p * second_row
            outs.append(acc[r0:r0 + HEAD_DIM, cols] * inv[:, cols])
        store(jnp.concatenate(outs, axis=0))

    return scores, weighted_values


def _mem_units(qm_ref, rows, mkv_ref, mvt_ref, masks, store_pair):
    def q2(t):
        qm = qm_ref[rows, :]
        return jnp.concatenate([qm * masks[2 * t], qm * masks[2 * t + 1]], axis=0)

    return [_pair_unit(lambda: mkv_ref[:, :MW], functools.partial(q2, t),
                       lambda t=t: mvt_ref[t * LANES:(t + 1) * LANES, :], HEAD_DIM,
                       (None, None), (None, None), functools.partial(store_pair, t))
            for t in range(N_MEM_HEADS // 2)]


def _mem_kv_kernel(mem_ref, g_ref, w_ref, w0_ref, o_ref, w0_out_ref, *, job):
    mem_n = _rms(mem_ref[...], g_ref[...])
    o_ref[...] = jnp.dot(mem_n.astype(BF16), w_ref[...].astype(BF16),
                         preferred_element_type=F32).astype(BF16)
    _cast_chunk(pl.program_id(0), job, w0_ref, w0_out_ref)


def _mem_kv(mem, norm_mem, w_mem_kv, w_in0, regroup0):
    depth, d, w = w_mem_kv.shape
    n_mem = mem.shape[0]
    job = _CastJob(w_in0, 0, depth, regroup0)
    return pl.pallas_call(
        functools.partial(_mem_kv_kernel, job=job),
        grid=(depth,),
        in_specs=[_resident((n_mem, d)), _resident((1, d)),
                  pl.BlockSpec((None, d, w), lambda i: (i, 0, 0)), job.in_spec],
        out_specs=[pl.BlockSpec((None, n_mem, w), lambda i: (i, 0, 0)), job.out_spec],
        out_shape=[jax.ShapeDtypeStruct((depth, n_mem, w), BF16), job.out_shape],
        name="mem_kv",
    )(mem, norm_mem.reshape(1, d), w_mem_kv, w_in0)


def _query_col_scale(width, n_q, n_qm):
    cs = np.ones((1, width), np.float32)
    cs[:, :n_q] = Q_SCALE
    cs[:, width - n_qm:] = Q_SCALE
    return jnp.asarray(cs)


def _proj_kernel(x_ref, g_ref, w_ref, cs_ref, o_ref):
    tiles = [slice(k * SUB_TILE, (k + 1) * SUB_TILE) for k in range(PROJ_TILE // SUB_TILE)]
    h = _rms(x_ref[tiles[0], :], g_ref[...]).astype(BF16)
    for k, rows in enumerate(tiles):
        h_next = None
        if k + 1 < len(tiles):
            h_next = _rms(x_ref[tiles[k + 1], :], g_ref[...]).astype(BF16)
        o_ref[rows, :] = (jnp.dot(h, w_ref[...], preferred_element_type=F32)
                          * cs_ref[...]).astype(BF16)
        h = h_next


def _proj(x, g, layer, w):
    s, d = x.shape
    n = w.shape[1]
    return pl.pallas_call(
        _proj_kernel,
        grid=(s // PROJ_TILE,),
        in_specs=[pl.BlockSpec((PROJ_TILE, d), lambda i: (i, 0)),
                  _layer_block(g, layer), _resident(w.shape), _resident((1, n))],
        out_specs=pl.BlockSpec((PROJ_TILE, n), lambda i: (i, 0)),
        out_shape=jax.ShapeDtypeStruct((s, n), BF16),
        compiler_params=pltpu.CompilerParams(
            dimension_semantics=("arbitrary",), vmem_limit_bytes=VMEM_LIMIT_BYTES),
        name="in_proj",
    )(x, g, w, _query_col_scale(n, QW, MW))


def _alibi_slope(h):
    return 2.0 ** (-8.0 * (h + 1) / N_Q_HEADS)


def _q_regroup_moves_a(width):
    heads = [GROUP_A * j + g for g in range(GROUP_A) for j in range(N_KV_HEADS_A)]
    moves = [(t * LANES, [heads[2 * t] * HEAD_DIM, heads[2 * t + 1] * HEAD_DIM], HEAD_DIM)
             for t in range(N_Q_HEADS // 2)]
    return moves + [(QW, [QW], width - QW)]


def _slab_block_a(i, nb):
    return jnp.clip(i * SUB_BLOCKS_A - 1, 0, nb - (SUB_BLOCKS_A + 2))


def _attn_a_kernel(sink_ref, q_ref, qm_ref, kv_ref, mkv_ref, *rest, nb, jobs):
    n = len(jobs)
    y_ref, bias_ref, mvt_ref, vbt_ref = rest[n], rest[2 * n + 1], rest[2 * n + 2], rest[2 * n + 3]
    i = pl.program_id(0)
    band = 3 * BLOCK_A
    for job, src_ref, dst_ref in zip(jobs, rest[:n], rest[n + 1:2 * n + 1]):
        _cast_chunk(i, job, src_ref, dst_ref)

    @pl.when(i == 0)
    def _():
        c = lax.broadcasted_iota(jnp.int32, (band, BLOCK_A), 0)
        r = lax.broadcasted_iota(jnp.int32, (band, BLOCK_A), 1)
        for var in range(3):
            dist = jnp.abs(r - c + var * BLOCK_A)
            allowed = dist <= WINDOW
            distf = dist.astype(F32)
            for h in range(N_Q_HEADS):
                bias_ref[var, h] = jnp.where(allowed, -(_alibi_slope(h) * LOG2E) * distf, NEG_INF)

        mvt_ref[...] = mkv_ref[:, MW:].T

    slab_blk = _slab_block_a(i, nb)
    masks = _slot_masks(BLOCK_A)

    def store_pair(rows, t, z):
        y_ref[rows, t * LANES:(t + 1) * LANES] = z.T.astype(BF16)

    def token_unit(s, t, off, var):
        rows = slice(s * BLOCK_A, (s + 1) * BLOCK_A)
        heads = (2 * t, 2 * t + 1)
        kvh = [h // GROUP_A for h in heads]

        def k_all():
            if t == 0:
                vbt_ref[s] = kv_ref[pl.ds(off, band), KVW_A:].T
            return kv_ref[pl.ds(off, band), :KVW_A]

        def q2():
            return jnp.concatenate(
                [q_ref[rows, (h % GROUP_A) * QUAD:(h % GROUP_A + 1) * QUAD] * masks[j]
                 for h, j in zip(heads, kvh)], axis=0)

        return _pair_unit(
            k_all, q2, lambda: vbt_ref[s, kvh[0] * HEAD_DIM:(kvh[1] + 1) * HEAD_DIM, :],
            (kvh[1] - kvh[0]) * HEAD_DIM, [sink_ref[h] * LOG2E for h in heads],
            [lambda keys, h=h: bias_ref[var, h, keys, :] for h in heads],
            functools.partial(store_pair, rows, t))

    units = []
    for s in range(SUB_BLOCKS_A):
        g_blk = i * SUB_BLOCKS_A + s
        band_blk = jnp.clip(g_blk - 1, 0, nb - 3)
        off = pl.multiple_of((band_blk - slab_blk) * BLOCK_A, BLOCK_A)
        var = g_blk - band_blk
        rows = slice(s * BLOCK_A, (s + 1) * BLOCK_A)
        units += [token_unit(s, t, off, var) for t in range(N_Q_HEADS // 2)]
        units += _mem_units(qm_ref, rows, mkv_ref, mvt_ref, masks,
                            lambda t, z, rows=rows: store_pair(rows, N_Q_HEADS // 2 + t, z))
    _run_pipelined(units, STAGE_DELAY_A)


def _attn_a(proj, mkv, layer, sink, casts):
    seq, _ = proj.shape
    nb = seq // BLOCK_A
    tq = SUB_BLOCKS_A * BLOCK_A
    slab = (SUB_BLOCKS_A + 2) * BLOCK_A
    n_steps = nb // SUB_BLOCKS_A
    jobs = [_CastJob(w, l, n_steps, rg) for w, l, rg in casts]
    outs = pl.pallas_call(
        functools.partial(_attn_a_kernel, nb=nb, jobs=jobs),
        grid=(n_steps,),
        in_specs=[
            pl.BlockSpec(memory_space=pltpu.SMEM),
            pl.BlockSpec((tq, QW), lambda i: (i, 0)),
            pl.BlockSpec((tq, MW), lambda i: (i, (QW + 2 * KVW_A) // MW)),
            pl.BlockSpec((pl.Element(slab), pl.Element(2 * KVW_A)),
                         lambda i: (_slab_block_a(i, nb) * BLOCK_A, QW)),
            _layer_block(mkv, layer),
        ] + [job.in_spec for job in jobs],
        out_specs=[pl.BlockSpec((tq, QW + MW), lambda i: (i, 0))] + [job.out_spec for job in jobs],
        out_shape=[jax.ShapeDtypeStruct((seq, QW + MW), BF16)] + [job.out_shape for job in jobs],
        scratch_shapes=[pltpu.VMEM((3, N_Q_HEADS, 3 * BLOCK_A, BLOCK_A), F32),
                        pltpu.VMEM((MW, mkv.shape[1]), BF16),
                        pltpu.VMEM((SUB_BLOCKS_A, KVW_A, 3 * BLOCK_A), BF16)],
        compiler_params=pltpu.CompilerParams(
            dimension_semantics=("arbitrary",), vmem_limit_bytes=VMEM_LIMIT_BYTES),
        name="attn_a",
    )(sink, proj, proj, proj, mkv, *[w for w, _, _ in casts])
    return outs[0], outs[1:]


N_DR = 2 * NA_ROWS - 1
N_DC = 2 * NA_COLS - 1
N_QUADS = N_Q_HEADS // 4


def _bias_table_kernel(rpb_ref, o_ref):
    li = pl.program_id(0)
    hq = pl.program_id(1)
    kc = lax.broadcasted_iota(jnp.int32, (GRID_W, QUAD), 0)
    lane = lax.broadcasted_iota(jnp.int32, (GRID_W, QUAD), 1)
    qc = lane % GRID_W
    col_start = jnp.clip(qc - NA_COLS // 2, 0, GRID_W - NA_COLS)
    in_win = (kc >= col_start) & (kc < col_start + NA_COLS)
    idx = jnp.clip(kc - qc, -(NA_COLS - 1), NA_COLS - 1) + NA_COLS - 1
    slot_row = _head_slot((1, QUAD))
    base = (li * N_Q_HEADS + 4 * hq) * (N_DR * N_DC)

    def body(d, carry):
        acc = jnp.zeros((GRID_W, QUAD), F32)
        for v in range(N_DC):
            a = [rpb_ref[base + hl * (N_DR * N_DC) + d * N_DC + v] for hl in range(4)]
            val = jnp.where(slot_row == 0, a[0],
                            jnp.where(slot_row == 1, a[1], jnp.where(slot_row == 2, a[2], a[3])))
            acc = jnp.where(idx == v, val, acc)
        o_ref[pl.ds(pl.multiple_of(d * GRID_W, GRID_W), GRID_W), :] = jnp.where(in_win, acc * LOG2E, NEG_INF)
        return carry

    lax.fori_loop(0, N_DR, body, 0)


def _bias_tables(rpb_b):
    n_layers = rpb_b.shape[0]
    return pl.pallas_call(
        _bias_table_kernel,
        grid=(n_layers, N_QUADS),
        in_specs=[pl.BlockSpec(memory_space=pltpu.SMEM)],
        out_specs=pl.BlockSpec((None, None, N_DR * GRID_W, QUAD),
                               lambda li, hq: (li, hq, 0, 0)),
        out_shape=jax.ShapeDtypeStruct((n_layers, N_QUADS, N_DR * GRID_W, QUAD), F32),
        name="rpb_table",
    )(rpb_b.reshape(-1))


def _slab_start_row(b, rows):
    return jnp.clip(b * ROWS_PER_STEP_B - NA_ROWS // 2, 0, rows - 2 * ROWS_PER_STEP_B)


def _attn_b_kernel(q_ref, qm_ref, kv_ref, bias_ref, mkv_ref, *rest, rows, jobs):
    n = len(jobs)
    y_ref, mvt_ref = rest[n], rest[2 * n + 1]
    b = pl.program_id(0)
    for job, src_ref, dst_ref in zip(jobs, rest[:n], rest[n + 1:2 * n + 1]):
        _cast_chunk(b, job, src_ref, dst_ref)
    win = NA_ROWS * GRID_W
    slot = _head_slot((GRID_W, QUAD))
    slab0 = _slab_start_row(b, rows)

    @pl.when(b == 0)
    def _():
        mvt_ref[...] = mkv_ref[:, MW:].T

    masks64 = _slot_masks(GRID_W)
    masks128 = _slot_masks(LANES)

    def quad_unit(j, hq):
        r = b * ROWS_PER_STEP_B + j
        row_start = jnp.clip(r - NA_ROWS // 2, 0, rows - NA_ROWS)
        koff = pl.multiple_of((row_start - slab0) * GRID_W, GRID_W)
        boff = pl.multiple_of((row_start - r + NA_ROWS - 1) * GRID_W, GRID_W)
        qrows = slice(j * GRID_W, (j + 1) * GRID_W)
        lo = hq * QUAD

        def scores(_):
            q4 = q_ref[qrows, lo:lo + QUAD]
            qd = jnp.concatenate([q4 * masks64[hl] for hl in range(4)], axis=0)
            k4 = kv_ref[pl.ds(koff, win), lo:lo + QUAD]
            st = lax.dot_general(k4, qd, _NT, preferred_element_type=F32)
            st = st + bias_ref[hq, pl.ds(boff, win), :]
            return st, jnp.max(st, axis=0, keepdims=True)

        def softmax(state):
            st, m = state
            e = jnp.exp2(st - m)
            return e.astype(BF16), jnp.sum(e, axis=0, keepdims=True)

        def weighted_values(state):
            p, l = state
            v4 = kv_ref[pl.ds(koff, win), QW + lo:QW + lo + QUAD]
            o2 = lax.dot_general(p, v4, _TN, preferred_element_type=F32)
            inv = jnp.broadcast_to(1.0 / l, (LANES, QUAD)).T
            y4 = o2[:GRID_W] * jnp.tile(inv[:GRID_W], (1, 2))
            for hl in range(1, 4):
                part = o2[hl * GRID_W:(hl + 1) * GRID_W] * jnp.tile(inv[hl * GRID_W:(hl + 1) * GRID_W], (1, 2))
                y4 = jnp.where(slot == hl, part, y4)
            y_ref[qrows, lo:lo + QUAD] = y4.astype(BF16)

        return scores, softmax, weighted_values

    def store_mem_pair(blk, t, z):
        y_ref[blk, QW + t * LANES:QW + (t + 1) * LANES] = z.T.astype(BF16)

    units = [quad_unit(j, hq) for j in range(ROWS_PER_STEP_B) for hq in range(N_QUADS)]
    for s in range(ROWS_PER_STEP_B * GRID_W // LANES):
        blk = slice(s * LANES, (s + 1) * LANES)
        units += _mem_units(qm_ref, blk, mkv_ref, mvt_ref, masks128,
                            functools.partial(store_mem_pair, blk))
    _run_pipelined(units, STAGE_DELAY_B)


def _attn_b(proj, mkv, layer, bias, bias_layer, casts):
    seq, _ = proj.shape
    rows = seq // GRID_W
    tq = ROWS_PER_STEP_B * GRID_W
    slab = 2 * ROWS_PER_STEP_B * GRID_W
    n_steps = rows // ROWS_PER_STEP_B
    jobs = [_CastJob(w, l, n_steps, rg) for w, l, rg in casts]
    outs = pl.pallas_call(
        functools.partial(_attn_b_kernel, rows=rows, jobs=jobs),
        grid=(n_steps,),
        in_specs=[
            pl.BlockSpec((tq, QW), lambda b: (b, 0)),
            pl.BlockSpec((tq, MW), lambda b: (b, 3 * QW // MW)),
            pl.BlockSpec((pl.Element(slab), pl.Element(2 * QW)),
                         lambda b: (_slab_start_row(b, rows) * GRID_W, QW)),
            _layer_block(bias, bias_layer),
            _layer_block(mkv, layer),
        ] + [job.in_spec for job in jobs],
        out_specs=[pl.BlockSpec((tq, QW + MW), lambda b: (b, 0))] + [job.out_spec for job in jobs],
        out_shape=[jax.ShapeDtypeStruct((seq, QW + MW), BF16)] + [job.out_shape for job in jobs],
        scratch_shapes=[pltpu.VMEM((MW, mkv.shape[1]), BF16)],
        compiler_params=pltpu.CompilerParams(
            dimension_semantics=("arbitrary",), vmem_limit_bytes=VMEM_LIMIT_BYTES),
        name="attn_b",
    )(proj, proj, proj, bias, mkv, *[w for w, _, _ in casts])
    return outs[0], outs[1:]


def _post_kernel(x_ref, y_ref, wo_ref, gf_ref, wgu_ref, wd_ref, gn_ref, *rest, d_ff, last):
    tiles = [slice(k * SUB_TILE, (k + 1) * SUB_TILE) for k in range(x_ref.shape[0] // SUB_TILE)]
    x1 = [x_ref[rows, :] + jnp.dot(y_ref[rows, :], wo_ref[...], preferred_element_type=F32)
          for rows in tiles]
    act = []
    for k in range(len(tiles)):
        h = _rms(x1[k], gf_ref[...]).astype(BF16)
        gu = jnp.dot(h, wgu_ref[...], preferred_element_type=F32)
        g = gu[:, :d_ff]
        u = gu[:, d_ff:]
        act.append((g * (1.0 / (1.0 + jnp.exp(-g))) * u).astype(BF16))
    x2 = [x1[k] + jnp.dot(act[k], wd_ref[...], preferred_element_type=F32)
          for k in range(len(tiles))]
    if last:
        (o_ref,) = rest
        for k, rows in enumerate(tiles):
            o_ref[rows, :] = _rms(x2[k], gn_ref[...])
    else:
        wn_ref, cs_ref, o_ref, p_ref = rest
        for k, rows in enumerate(tiles):
            o_ref[rows, :] = x2[k]
            hn = _rms(x2[k], gn_ref[...]).astype(BF16)
            p_ref[rows, :] = (jnp.dot(hn, wn_ref[...], preferred_element_type=F32)
                              * cs_ref[...]).astype(BF16)


def _post(x, y, w_out, g_ffn, layer, w_gu, w_d, g_next, next_layer, w_next):
    s, d = x.shape
    d_ff = w_d.shape[0]
    last = w_next is None
    row_tile = LAST_ROW_TILE if last else ROW_TILE
    row = lambda i: (i, 0)
    in_specs = [pl.BlockSpec((row_tile, d), row), pl.BlockSpec((row_tile, y.shape[1]), row),
                _resident(w_out.shape), _layer_block(g_ffn, layer), _resident(w_gu.shape),
                _resident(w_d.shape), _layer_block(g_next, next_layer)]
    args = [x, y, w_out, g_ffn, w_gu, w_d, g_next]
    out_specs = [pl.BlockSpec((row_tile, d), row)]
    out_shape = [jax.ShapeDtypeStruct((s, d), F32)]
    if not last:
        n = w_next.shape[1]
        in_specs += [_resident(w_next.shape), _resident((1, n))]
        args += [w_next, _query_col_scale(n, QW, MW)]
        out_specs.append(pl.BlockSpec((row_tile, n), row))
        out_shape.append(jax.ShapeDtypeStruct((s, n), BF16))
    outs = pl.pallas_call(
        functools.partial(_post_kernel, d_ff=d_ff, last=last),
        grid=(s // row_tile,),
        in_specs=in_specs,
        out_specs=out_specs,
        out_shape=out_shape,
        compiler_params=pltpu.CompilerParams(
            dimension_semantics=("arbitrary",), vmem_limit_bytes=VMEM_LIMIT_BYTES),
        name="post_last" if last else "post",
    )(*args)
    return (outs[0], None) if last else (outs[0], outs[1])


def kernel(x, mem, norm_mix, norm_ffn, norm_mem, norm_final, w_in_a, sink_a, w_in_b, rpb_b,
           w_mem_kv, w_out, w_gate_up, w_down):
    batch, seq, d = x.shape
    assert batch == 1 and mem.shape[0] == 1
    assert seq % (ROWS_PER_STEP_B * GRID_W) == 0 and seq % ROW_TILE == 0 and seq % PROJ_TILE == 0 and seq % LAST_ROW_TILE == 0
    assert seq % (SUB_BLOCKS_A * BLOCK_A) == 0 and seq // BLOCK_A >= SUB_BLOCKS_A + 2
    depth = w_out.shape[0]
    xs = x.reshape(seq, d)
    w_in = (w_in_a, w_in_b)
    regroup = (_q_regroup_moves_a(w_in_a.shape[2]), None)
    g_mix = norm_mix.reshape(depth, 1, d)
    g_ffn = norm_ffn.reshape(depth, 1, d)
    g_final = norm_final.reshape(1, 1, d)

    mkv, w_in0 = _mem_kv(mem.reshape(mem.shape[1], d), norm_mem, w_mem_kv, w_in[0], regroup[0])
    bias = _bias_tables(rpb_b)

    proj = _proj(xs, g_mix, 0, w_in0)
    for i in range(depth):
        last = i == depth - 1
        nxt = (i + 1) % 2
        casts = [(w_out, i, None), (w_gate_up, i, None), (w_down, i, None)]
        if not last:
            casts.append((w_in[nxt], (i + 1) // 2, regroup[nxt]))
        if i % 2 == 0:
            y, wb = _attn_a(proj, mkv, i, sink_a[i // 2], casts)
        else:
            y, wb = _attn_b(proj, mkv, i, bias, i // 2, casts)
        if last:
            xs, proj = _post(xs, y, wb[0], g_ffn, i, wb[1], wb[2], g_final, 0, None)
        else:
            xs, proj = _post(xs, y, wb[0], g_ffn, i, wb[1], wb[2], g_mix, i + 1, wb[3])
    return xs.reshape(batch, seq, d)
```

```python
import functools

import numpy as np

import jax
import jax.numpy as jnp
from jax import lax
from jax.experimental import pallas as pl
from jax.experimental.pallas import tpu as pltpu

F32 = jnp.float32
BF16 = jnp.bfloat16

HEAD_DIM = 64
N_Q_HEADS = 12
N_KV_HEADS_A = 4
GROUP_A = N_Q_HEADS // N_KV_HEADS_A
N_MEM_HEADS = 4
WINDOW = 128
BLOCK_A = 128
SUB_BLOCKS_A = 8
GRID_W = 64
NA_ROWS = 8
NA_COLS = 16
ROWS_PER_STEP_B = 16
EPS = 1e-6
NEG_INF = -1e30
LOG2E = 1.4426950408889634
Q_SCALE = HEAD_DIM ** -0.5 * LOG2E

LANES = 128
BF16_ROWS = 16
QW = N_Q_HEADS * HEAD_DIM
KVW_A = N_KV_HEADS_A * HEAD_DIM
MW = N_MEM_HEADS * HEAD_DIM
QUAD = 4 * HEAD_DIM

VMEM_LIMIT_BYTES = 56 * 1024 * 1024
ROW_TILE = 512
LAST_ROW_TILE = 1024
PROJ_TILE = 2048
SUB_TILE = 256

_NT = (((1,), (1,)), ((), ()))
_TN = (((0,), (0,)), ((), ()))


def _resident(shape):
    n = len(shape)
    return pl.BlockSpec(shape, lambda *_: (0,) * n)


def _layer_block(arr, layer):
    n = arr.ndim - 1
    return pl.BlockSpec((None,) + arr.shape[1:], lambda *_: (layer,) + (0,) * n)


class _CastJob:
    def __init__(self, w, layer, n_steps, regroup=None):
        _, n_rows, n_cols = w.shape
        self.regroup = regroup
        self.n_chunks = n_steps
        while n_rows % (self.n_chunks * BF16_ROWS):
            self.n_chunks //= 2
        rc = n_rows // self.n_chunks
        last = self.n_chunks - 1
        self.in_spec = pl.BlockSpec((None, rc, n_cols), lambda s: (layer, jnp.minimum(s, last), 0))
        self.out_spec = pl.BlockSpec((rc, n_cols), lambda s: (jnp.minimum(s, last), 0))
        self.out_shape = jax.ShapeDtypeStruct((n_rows, n_cols), BF16)


def _cast_chunk(step, job, src_ref, dst_ref):
    @pl.when(step < job.n_chunks)
    def _():
        blk = src_ref[...]
        if job.regroup is None:
            dst_ref[...] = blk.astype(BF16)
        else:
            for dst, srcs, width in job.regroup:
                parts = [blk[:, src:src + width] for src in srcs]
                val = parts[0] if len(parts) == 1 else jnp.concatenate(parts, axis=1)
                dst_ref[:, dst:dst + len(srcs) * width] = val.astype(BF16)


def _rms(x, g):
    return x * lax.rsqrt(jnp.mean(x * x, axis=-1, keepdims=True) + EPS) * g


def _head_slot(shape):
    return lax.broadcasted_iota(jnp.int32, shape, 1) // HEAD_DIM


def _slot_masks(rows):
    slot = _head_slot((rows, QUAD))
    return [(slot == j).astype(BF16) for j in range(QUAD // HEAD_DIM)]


STAGE_DELAY_A = (0, 2)
STAGE_DELAY_B = (0, 1, 1)


def _run_pipelined(units, delays):
    state = {}
    for t in range(len(units) + delays[-1]):
        for k, delay in enumerate(delays):
            u = t - delay
            if 0 <= u < len(units):
                stages = units[u]
                if len(stages) == len(delays):
                    state[u] = stages[k](state.get(u))
                elif k in (0, len(delays) - 1):
                    state[u] = stages[min(k, 1)](state.get(u))


def _pair_unit(k_all, q2, vt2, second_row, sinks, biases, store):
    def scores(_):
        return lax.dot_general(k_all(), q2(), _NT, preferred_element_type=F32)

    def weighted_values(st):
        v = vt2()
        n_v = v.shape[0]
        lhs = jnp.concatenate([v, jnp.ones((BF16_ROWS, v.shape[1]), BF16)], axis=0)
        sink_row = None
        if sinks[0] is not None:
            sink_row = jnp.concatenate([jnp.full((1, LANES), s_, F32) for s_ in sinks], axis=1)
        m = sink_row if sink_row is not None else jnp.full((1, 2 * LANES), NEG_INF, F32)
        acc = None
        for c in range(st.shape[0] // LANES):
            keys = slice(c * LANES, (c + 1) * LANES)
            s_c = st[keys, :]
            if biases[0] is not None:
                s_c = s_c + jnp.concatenate([b(keys) for b in biases], axis=1)
            m_new = jnp.maximum(m, jnp.max(s_c, axis=0, keepdims=True))
            e = jnp.exp2(s_c - m_new).astype(BF16)
            part = jnp.dot(lhs[:, keys], e, preferred_element_type=F32)
            acc = part if acc is None else acc * jnp.exp2(m - m_new) + part
            m = m_new
        l = acc[n_v:n_v + 1, :]
        if sink_row is not None:
            l = l + jnp.exp2(sink_row - m)
        inv = 1.0 / l
        outs = []
        for p in range(2):
            cols = slice(p * LANES, (p + 1) * LANES)
            r0 = p * second_row
            outs.append(acc[r0:r0 + HEAD_DIM, cols] * inv[:, cols])
        store(jnp.concatenate(outs, axis=0))

    return scores, weighted_values


def _mem_units(qm_ref, rows, mkv_ref, mvt_ref, masks, store_pair):
    def q2(t):
        qm = qm_ref[rows, :]
        return jnp.concatenate([qm * masks[2 * t], qm * masks[2 * t + 1]], axis=0)

    return [_pair_unit(lambda: mkv_ref[:, :MW], functools.partial(q2, t),
                       lambda t=t: mvt_ref[t * LANES:(t + 1) * LANES, :], HEAD_DIM,
                       (None, None), (None, None), functools.partial(store_pair, t))
            for t in range(N_MEM_HEADS // 2)]


def _mem_kv_kernel(mem_ref, g_ref, w_ref, w0_ref, o_ref, w0_out_ref, *, job):
    mem_n = _rms(mem_ref[...], g_ref[...])
    o_ref[...] = jnp.dot(mem_n.astype(BF16), w_ref[...].astype(BF16),
                         preferred_element_type=F32).astype(BF16)
    _cast_chunk(pl.program_id(0), job, w0_ref, w0_out_ref)


def _mem_kv(mem, norm_mem, w_mem_kv, w_in0, regroup0):
    depth, d, w = w_mem_kv.shape
    n_mem = mem.shape[0]
    job = _CastJob(w_in0, 0, depth, regroup0)
    return pl.pallas_call(
        functools.partial(_mem_kv_kernel, job=job),
        grid=(depth,),
        in_specs=[_resident((n_mem, d)), _resident((1, d)),
                  pl.BlockSpec((None, d, w), lambda i: (i, 0, 0)), job.in_spec],
        out_specs=[pl.BlockSpec((None, n_mem, w), lambda i: (i, 0, 0)), job.out_spec],
        out_shape=[jax.ShapeDtypeStruct((depth, n_mem, w), BF16), job.out_shape],
        name="mem_kv",
    )(mem, norm_mem.reshape(1, d), w_mem_kv, w_in0)


def _query_col_scale(width, n_q, n_qm):
    cs = np.ones((1, width), np.float32)
    cs[:, :n_q] = Q_SCALE
    cs[:, width - n_qm:] = Q_SCALE
    return jnp.asarray(cs)


def _proj_kernel(x_ref, g_ref, w_ref, cs_ref, o_ref):
    tiles = [slice(k * SUB_TILE, (k + 1) * SUB_TILE) for k in range(PROJ_TILE // SUB_TILE)]
    h = _rms(x_ref[tiles[0], :], g_ref[...]).astype(BF16)
    for k, rows in enumerate(tiles):
        h_next = None
        if k + 1 < len(tiles):
            h_next = _rms(x_ref[tiles[k + 1], :], g_ref[...]).astype(BF16)
        o_ref[rows, :] = (jnp.dot(h, w_ref[...], preferred_element_type=F32)
                          * cs_ref[...]).astype(BF16)
        h = h_next


def _proj(x, g, layer, w):
    s, d = x.shape
    n = w.shape[1]
    return pl.pallas_call(
        _proj_kernel,
        grid=(s // PROJ_TILE,),
        in_specs=[pl.BlockSpec((PROJ_TILE, d), lambda i: (i, 0)),
                  _layer_block(g, layer), _resident(w.shape), _resident((1, n))],
        out_specs=pl.BlockSpec((PROJ_TILE, n), lambda i: (i, 0)),
        out_shape=jax.ShapeDtypeStruct((s, n), BF16),
        compiler_params=pltpu.CompilerParams(
            dimension_semantics=("arbitrary",), vmem_limit_bytes=VMEM_LIMIT_BYTES),
        name="in_proj",
    )(x, g, w, _query_col_scale(n, QW, MW))


def _alibi_slope(h):
    return 2.0 ** (-8.0 * (h + 1) / N_Q_HEADS)


def _q_regroup_moves_a(width):
    heads = [GROUP_A * j + g for g in range(GROUP_A) for j in range(N_KV_HEADS_A)]
    moves = [(t * LANES, [heads[2 * t] * HEAD_DIM, heads[2 * t + 1] * HEAD_DIM], HEAD_DIM)
             for t in range(N_Q_HEADS // 2)]
    return moves + [(QW, [QW], width - QW)]


def _slab_block_a(i, nb):
    return jnp.clip(i * SUB_BLOCKS_A - 1, 0, nb - (SUB_BLOCKS_A + 2))


def _attn_a_kernel(sink_ref, q_ref, qm_ref, kv_ref, mkv_ref, *rest, nb, jobs):
    n = len(jobs)
    y_ref, bias_ref, mvt_ref, vbt_ref = rest[n], rest[2 * n + 1], rest[2 * n + 2], rest[2 * n + 3]
    i = pl.program_id(0)
    band = 3 * BLOCK_A
    for job, src_ref, dst_ref in zip(jobs, rest[:n], rest[n + 1:2 * n + 1]):
        _cast_chunk(i, job, src_ref, dst_ref)

    @pl.when(i == 0)
    def _():
        c = lax.broadcasted_iota(jnp.int32, (band, BLOCK_A), 0)
        r = lax.broadcasted_iota(jnp.int32, (band, BLOCK_A), 1)
        for var in range(3):
            dist = jnp.abs(r - c + var * BLOCK_A)
            allowed = dist <= WINDOW
            distf = dist.astype(F32)
            for h in range(N_Q_HEADS):
                bias_ref[var, h] = jnp.where(allowed, -(_alibi_slope(h) * LOG2E) * distf, NEG_INF)

        mvt_ref[...] = mkv_ref[:, MW:].T

    slab_blk = _slab_block_a(i, nb)
    masks = _slot_masks(BLOCK_A)

    def store_pair(rows, t, z):
        y_ref[rows, t * LANES:(t + 1) * LANES] = z.T.astype(BF16)

    def token_unit(s, t, off, var):
        rows = slice(s * BLOCK_A, (s + 1) * BLOCK_A)
        heads = (2 * t, 2 * t + 1)
        kvh = [h // GROUP_A for h in heads]

        def k_all():
            if t == 0:
                vbt_ref[s] = kv_ref[pl.ds(off, band), KVW_A:].T
            return kv_ref[pl.ds(off, band), :KVW_A]

        def q2():
            return jnp.concatenate(
                [q_ref[rows, (h % GROUP_A) * QUAD:(h % GROUP_A + 1) * QUAD] * masks[j]
                 for h, j in zip(heads, kvh)], axis=0)

        return _pair_unit(
            k_all, q2, lambda: vbt_ref[s, kvh[0] * HEAD_DIM:(kvh[1] + 1) * HEAD_DIM, :],
            (kvh[1] - kvh[0]) * HEAD_DIM, [sink_ref[h] * LOG2E for h in heads],
            [lambda keys, h=h: bias_ref[var, h, keys, :] for h in heads],
            functools.partial(store_pair, rows, t))

    units = []
    for s in range(SUB_BLOCKS_A):
        g_blk = i * SUB_BLOCKS_A + s
        band_blk = jnp.clip(g_blk - 1, 0, nb - 3)
        off = pl.multiple_of((band_blk - slab_blk) * BLOCK_A, BLOCK_A)
        var = g_blk - band_blk
        rows = slice(s * BLOCK_A, (s + 1) * BLOCK_A)
        units += [token_unit(s, t, off, var) for t in range(N_Q_HEADS // 2)]
        units += _mem_units(qm_ref, rows, mkv_ref, mvt_ref, masks,
                            lambda t, z, rows=rows: store_pair(rows, N_Q_HEADS // 2 + t, z))
    _run_pipelined(units, STAGE_DELAY_A)


def _attn_a(proj, mkv, layer, sink, casts):
    seq, _ = proj.shape
    nb = seq // BLOCK_A
    tq = SUB_BLOCKS_A * BLOCK_A
    slab = (SUB_BLOCKS_A + 2) * BLOCK_A
    n_steps = nb // SUB_BLOCKS_A
    jobs = [_CastJob(w, l, n_steps, rg) for w, l, rg in casts]
    outs = pl.pallas_call(
        functools.partial(_attn_a_kernel, nb=nb, jobs=jobs),
        grid=(n_steps,),
        in_specs=[
            pl.BlockSpec(memory_space=pltpu.SMEM),
            pl.BlockSpec((tq, QW), lambda i: (i, 0)),
            pl.BlockSpec((tq, MW), lambda i: (i, (QW + 2 * KVW_A) // MW)),
            pl.BlockSpec((pl.Element(slab), pl.Element(2 * KVW_A)),
                         lambda i: (_slab_block_a(i, nb) * BLOCK_A, QW)),
            _layer_block(mkv, layer),
        ] + [job.in_spec for job in jobs],
        out_specs=[pl.BlockSpec((tq, QW + MW), lambda i: (i, 0))] + [job.out_spec for job in jobs],
        out_shape=[jax.ShapeDtypeStruct((seq, QW + MW), BF16)] + [job.out_shape for job in jobs],
        scratch_shapes=[pltpu.VMEM((3, N_Q_HEADS, 3 * BLOCK_A, BLOCK_A), F32),
                        pltpu.VMEM((MW, mkv.shape[1]), BF16),
                        pltpu.VMEM((SUB_BLOCKS_A, KVW_A, 3 * BLOCK_A), BF16)],
        compiler_params=pltpu.CompilerParams(
            dimension_semantics=("arbitrary",), vmem_limit_bytes=VMEM_LIMIT_BYTES),
        name="attn_a",
    )(sink, proj, proj, proj, mkv, *[w for w, _, _ in casts])
    return outs[0], outs[1:]


N_DR = 2 * NA_ROWS - 1
N_DC = 2 * NA_COLS - 1
N_QUADS = N_Q_HEADS // 4


def _bias_table_kernel(rpb_ref, o_ref):
    li = pl.program_id(0)
    hq = pl.program_id(1)
    kc = lax.broadcasted_iota(jnp.int32, (GRID_W, QUAD), 0)
    lane = lax.broadcasted_iota(jnp.int32, (GRID_W, QUAD), 1)
    qc = lane % GRID_W
    col_start = jnp.clip(qc - NA_COLS // 2, 0, GRID_W - NA_COLS)
    in_win = (kc >= col_start) & (kc < col_start + NA_COLS)
    idx = jnp.clip(kc - qc, -(NA_COLS - 1), NA_COLS - 1) + NA_COLS - 1
    slot_row = _head_slot((1, QUAD))
    base = (li * N_Q_HEADS + 4 * hq) * (N_DR * N_DC)

    def body(d, carry):
        acc = jnp.zeros((GRID_W, QUAD), F32)
        for v in range(N_DC):
            a = [rpb_ref[base + hl * (N_DR * N_DC) + d * N_DC + v] for hl in range(4)]
            val = jnp.where(slot_row == 0, a[0],
                            jnp.where(slot_row == 1, a[1], jnp.where(slot_row == 2, a[2], a[3])))
            acc = jnp.where(idx == v, val, acc)
        o_ref[pl.ds(pl.multiple_of(d * GRID_W, GRID_W), GRID_W), :] = jnp.where(in_win, acc * LOG2E, NEG_INF)
        return carry

    lax.fori_loop(0, N_DR, body, 0)


def _bias_tables(rpb_b):
    n_layers = rpb_b.shape[0]
    return pl.pallas_call(
        _bias_table_kernel,
        grid=(n_layers, N_QUADS),
        in_specs=[pl.BlockSpec(memory_space=pltpu.SMEM)],
        out_specs=pl.BlockSpec((None, None, N_DR * GRID_W, QUAD),
                               lambda li, hq: (li, hq, 0, 0)),
        out_shape=jax.ShapeDtypeStruct((n_layers, N_QUADS, N_DR * GRID_W, QUAD), F32),
        name="rpb_table",
    )(rpb_b.reshape(-1))


SLAB_ROWS_B = ROWS_PER_STEP_B + NA_ROWS


def _slab_start_row(b, rows):
    return jnp.clip(b * ROWS_PER_STEP_B - NA_ROWS // 2, 0, rows - SLAB_ROWS_B)


def _attn_b_kernel(q_ref, qm_ref, kv_ref, bias_ref, mkv_ref, *rest, rows, jobs):
    n = len(jobs)
    y_ref, mvt_ref = rest[n], rest[2 * n + 1]
    b = pl.program_id(0)
    for job, src_ref, dst_ref in zip(jobs, rest[:n], rest[n + 1:2 * n + 1]):
        _cast_chunk(b, job, src_ref, dst_ref)
    win = NA_ROWS * GRID_W
    slot = _head_slot((GRID_W, QUAD))
    slab0 = _slab_start_row(b, rows)

    @pl.when(b == 0)
    def _():
        mvt_ref[...] = mkv_ref[:, MW:].T

    masks64 = _slot_masks(GRID_W)
    masks128 = _slot_masks(LANES)

    def quad_unit(j, hq):
        r = b * ROWS_PER_STEP_B + j
        row_start = jnp.clip(r - NA_ROWS // 2, 0, rows - NA_ROWS)
        koff = pl.multiple_of((row_start - slab0) * GRID_W, GRID_W)
        boff = pl.multiple_of((row_start - r + NA_ROWS - 1) * GRID_W, GRID_W)
        qrows = slice(j * GRID_W, (j + 1) * GRID_W)
        lo = hq * QUAD

        def scores(_):
            q4 = q_ref[qrows, lo:lo + QUAD]
            qd = jnp.concatenate([q4 * masks64[hl] for hl in range(4)], axis=0)
            k4 = kv_ref[pl.ds(koff, win), lo:lo + QUAD]
            st = lax.dot_general(k4, qd, _NT, preferred_element_type=F32)
            st = st + bias_ref[hq, pl.ds(boff, win), :]
            return st, jnp.max(st, axis=0, keepdims=True)

        def softmax(state):
            st, m = state
            e = jnp.exp2(st - m)
            return e.astype(BF16), jnp.sum(e, axis=0, keepdims=True)

        def weighted_values(state):
            p, l = state
            v4 = kv_ref[pl.ds(koff, win), QW + lo:QW + lo + QUAD]
            o2 = lax.dot_general(p, v4, _TN, preferred_element_type=F32)
            inv = jnp.broadcast_to(1.0 / l, (LANES, QUAD)).T
            y4 = o2[:GRID_W] * jnp.tile(inv[:GRID_W], (1, 2))
            for hl in range(1, 4):
                part = o2[hl * GRID_W:(hl + 1) * GRID_W] * jnp.tile(inv[hl * GRID_W:(hl + 1) * GRID_W], (1, 2))
                y4 = jnp.where(slot == hl, part, y4)
            y_ref[qrows, lo:lo + QUAD] = y4.astype(BF16)

        return scores, softmax, weighted_values

    def store_mem_pair(blk, t, z):
        y_ref[blk, QW + t * LANES:QW + (t + 1) * LANES] = z.T.astype(BF16)

    units = [quad_unit(j, hq) for j in range(ROWS_PER_STEP_B) for hq in range(N_QUADS)]
    for s in range(ROWS_PER_STEP_B * GRID_W // LANES):
        blk = slice(s * LANES, (s + 1) * LANES)
        units += _mem_units(qm_ref, blk, mkv_ref, mvt_ref, masks128,
                            functools.partial(store_mem_pair, blk))
    _run_pipelined(units, STAGE_DELAY_B)


def _attn_b(proj, mkv, layer, bias, bias_layer, casts):
    seq, _ = proj.shape
    rows = seq // GRID_W
    tq = ROWS_PER_STEP_B * GRID_W
    slab = SLAB_ROWS_B * GRID_W
    n_steps = rows // ROWS_PER_STEP_B
    jobs = [_CastJob(w, l, n_steps, rg) for w, l, rg in casts]
    outs = pl.pallas_call(
        functools.partial(_attn_b_kernel, rows=rows, jobs=jobs),
        grid=(n_steps,),
        in_specs=[
            pl.BlockSpec((tq, QW), lambda b: (b, 0)),
            pl.BlockSpec((tq, MW), lambda b: (b, 3 * QW // MW)),
            pl.BlockSpec((pl.Element(slab), pl.Element(2 * QW)),
                         lambda b: (_slab_start_row(b, rows) * GRID_W, QW)),
            _layer_block(bias, bias_layer),
            _layer_block(mkv, layer),
        ] + [job.in_spec for job in jobs],
        out_specs=[pl.BlockSpec((tq, QW + MW), lambda b: (b, 0))] + [job.out_spec for job in jobs],
        out_shape=[jax.ShapeDtypeStruct((seq, QW + MW), BF16)] + [job.out_shape for job in jobs],
        scratch_shapes=[pltpu.VMEM((MW, mkv.shape[1]), BF16)],
        compiler_params=pltpu.CompilerParams(
            dimension_semantics=("arbitrary",), vmem_limit_bytes=VMEM_LIMIT_BYTES),
        name="attn_b",
    )(proj, proj, proj, bias, mkv, *[w for w, _, _ in casts])
    return outs[0], outs[1:]


def _post_kernel(x_ref, y_ref, wo_ref, gf_ref, wgu_ref, wd_ref, gn_ref, *rest, d_ff, last):
    tiles = [slice(k * SUB_TILE, (k + 1) * SUB_TILE) for k in range(x_ref.shape[0] // SUB_TILE)]
    x1 = [x_ref[rows, :] + jnp.dot(y_ref[rows, :], wo_ref[...], preferred_element_type=F32)
          for rows in tiles]
    act = []
    for k in range(len(tiles)):
        h = _rms(x1[k], gf_ref[...]).astype(BF16)
        gu = jnp.dot(h, wgu_ref[...], preferred_element_type=F32)
        g = gu[:, :d_ff]
        u = gu[:, d_ff:]
        act.append((g * (1.0 / (1.0 + jnp.exp(-g))) * u).astype(BF16))
    x2 = [x1[k] + jnp.dot(act[k], wd_ref[...], preferred_element_type=F32)
          for k in range(len(tiles))]
    if last:
        (o_ref,) = rest
        for k, rows in enumerate(tiles):
            o_ref[rows, :] = _rms(x2[k], gn_ref[...])
    else:
        wn_ref, cs_ref, o_ref, p_ref = rest
        for k, rows in enumerate(tiles):
            o_ref[rows, :] = x2[k]
            hn = _rms(x2[k], gn_ref[...]).astype(BF16)
            p_ref[rows, :] = (jnp.dot(hn, wn_ref[...], preferred_element_type=F32)
                              * cs_ref[...]).astype(BF16)


def _post(x, y, w_out, g_ffn, layer, w_gu, w_d, g_next, next_layer, w_next):
    s, d = x.shape
    d_ff = w_d.shape[0]
    last = w_next is None
    row_tile = LAST_ROW_TILE if last else ROW_TILE
    row = lambda i: (i, 0)
    in_specs = [pl.BlockSpec((row_tile, d), row), pl.BlockSpec((row_tile, y.shape[1]), row),
                _resident(w_out.shape), _layer_block(g_ffn, layer), _resident(w_gu.shape),
                _resident(w_d.shape), _layer_block(g_next, next_layer)]
    args = [x, y, w_out, g_ffn, w_gu, w_d, g_next]
    out_specs = [pl.BlockSpec((row_tile, d), row)]
    out_shape = [jax.ShapeDtypeStruct((s, d), F32)]
    if not last:
        n = w_next.shape[1]
        in_specs += [_resident(w_next.shape), _resident((1, n))]
        args += [w_next, _query_col_scale(n, QW, MW)]
        out_specs.append(pl.BlockSpec((row_tile, n), row))
        out_shape.append(jax.ShapeDtypeStruct((s, n), BF16))
    outs = pl.pallas_call(
        functools.partial(_post_kernel, d_ff=d_ff, last=last),
        grid=(s // row_tile,),
        in_specs=in_specs,
        out_specs=out_specs,
        out_shape=out_shape,
        compiler_params=pltpu.CompilerParams(
            dimension_semantics=("arbitrary",), vmem_limit_bytes=VMEM_LIMIT_BYTES),
        name="post_last" if last else "post",
    )(*args)
    return (outs[0], None) if last else (outs[0], outs[1])


def kernel(x, mem, norm_mix, norm_ffn, norm_mem, norm_final, w_in_a, sink_a, w_in_b, rpb_b,
           w_mem_kv, w_out, w_gate_up, w_down):
    batch, seq, d = x.shape
    assert batch == 1 and mem.shape[0] == 1
    assert seq % (ROWS_PER_STEP_B * GRID_W) == 0 and seq % ROW_TILE == 0 and seq % PROJ_TILE == 0 and seq % LAST_ROW_TILE == 0
    assert seq % (SUB_BLOCKS_A * BLOCK_A) == 0 and seq // BLOCK_A >= SUB_BLOCKS_A + 2
    depth = w_out.shape[0]
    xs = x.reshape(seq, d)
    w_in = (w_in_a, w_in_b)
    regroup = (_q_regroup_moves_a(w_in_a.shape[2]), None)
    g_mix = norm_mix.reshape(depth, 1, d)
    g_ffn = norm_ffn.reshape(depth, 1, d)
    g_final = norm_final.reshape(1, 1, d)

    mkv, w_in0 = _mem_kv(mem.reshape(mem.shape[1], d), norm_mem, w_mem_kv, w_in[0], regroup[0])
    bias = _bias_tables(rpb_b)

    proj = _proj(xs, g_mix, 0, w_in0)
    for i in range(depth):
        last = i == depth - 1
        nxt = (i + 1) % 2
        casts = [(w_out, i, None), (w_gate_up, i, None), (w_down, i, None)]
        if not last:
            casts.append((w_in[nxt], (i + 1) // 2, regroup[nxt]))
        if i % 2 == 0:
            y, wb = _attn_a(proj, mkv, i, sink_a[i // 2], casts)
        else:
            y, wb = _attn_b(proj, mkv, i, bias, i // 2, casts)
        if last:
            xs, proj = _post(xs, y, wb[0], g_ffn, i, wb[1], wb[2], g_final, 0, None)
        else:
            xs, proj = _post(xs, y, wb[0], g_ffn, i, wb[1], wb[2], g_mix, i + 1, wb[3])
    return xs.reshape(batch, seq, d)
```

```python
import functools

import numpy as np

import jax
import jax.numpy as jnp
from jax import lax
from jax.experimental import pallas as pl
from jax.experimental.pallas import tpu as pltpu

F32 = jnp.float32
BF16 = jnp.bfloat16

HEAD_DIM = 64
N_Q_HEADS = 12
N_KV_HEADS_A = 4
GROUP_A = N_Q_HEADS // N_KV_HEADS_A
N_MEM_HEADS = 4
WINDOW = 128
BLOCK_A = 128
SUB_BLOCKS_A = 16
GRID_W = 64
NA_ROWS = 8
NA_COLS = 16
ROWS_PER_STEP_B = 16
EPS = 1e-6
NEG_INF = -1e30
LOG2E = 1.4426950408889634
Q_SCALE = HEAD_DIM ** -0.5 * LOG2E

LANES = 128
BF16_ROWS = 16
QW = N_Q_HEADS * HEAD_DIM
KVW_A = N_KV_HEADS_A * HEAD_DIM
MW = N_MEM_HEADS * HEAD_DIM
QUAD = 4 * HEAD_DIM

VMEM_LIMIT_BYTES = 56 * 1024 * 1024
ROW_TILE = 512
LAST_ROW_TILE = 1024
PROJ_TILE = 2048
SUB_TILE = 256

_NT = (((1,), (1,)), ((), ()))
_TN = (((0,), (0,)), ((), ()))


def _resident(shape):
    n = len(shape)
    return pl.BlockSpec(shape, lambda *_: (0,) * n)


def _layer_block(arr, layer):
    n = arr.ndim - 1
    return pl.BlockSpec((None,) + arr.shape[1:], lambda *_: (layer,) + (0,) * n)


class _CastJob:
    def __init__(self, w, layer, n_steps, regroup=None):
        _, n_rows, n_cols = w.shape
        self.regroup = regroup
        self.n_chunks = n_steps
        while n_rows % (self.n_chunks * BF16_ROWS):
            self.n_chunks //= 2
        rc = n_rows // self.n_chunks
        last = self.n_chunks - 1
        self.in_spec = pl.BlockSpec((None, rc, n_cols), lambda s: (layer, jnp.minimum(s, last), 0))
        self.out_spec = pl.BlockSpec((rc, n_cols), lambda s: (jnp.minimum(s, last), 0))
        self.out_shape = jax.ShapeDtypeStruct((n_rows, n_cols), BF16)


def _cast_chunk(step, job, src_ref, dst_ref):
    @pl.when(step < job.n_chunks)
    def _():
        blk = src_ref[...]
        if job.regroup is None:
            dst_ref[...] = blk.astype(BF16)
        else:
            for dst, srcs, width in job.regroup:
                parts = [blk[:, src:src + width] for src in srcs]
                val = parts[0] if len(parts) == 1 else jnp.concatenate(parts, axis=1)
                dst_ref[:, dst:dst + len(srcs) * width] = val.astype(BF16)


def _rms(x, g):
    return x * lax.rsqrt(jnp.mean(x * x, axis=-1, keepdims=True) + EPS) * g


def _head_slot(shape):
    return lax.broadcasted_iota(jnp.int32, shape, 1) // HEAD_DIM


def _slot_masks(rows):
    slot = _head_slot((rows, QUAD))
    return [(slot == j).astype(BF16) for j in range(QUAD // HEAD_DIM)]


STAGE_DELAY_A = (0, 2)
STAGE_DELAY_B = (0, 1, 1)


def _run_pipelined(units, delays):
    state = {}
    for t in range(len(units) + delays[-1]):
        for k, delay in enumerate(delays):
            u = t - delay
            if 0 <= u < len(units):
                stages = units[u]
                if len(stages) == len(delays):
                    state[u] = stages[k](state.get(u))
                elif k in (0, len(delays) - 1):
                    state[u] = stages[min(k, 1)](state.get(u))


def _pair_unit(k_all, q2, vt2, second_row, sinks, biases, store):
    def scores(_):
        return lax.dot_general(k_all(), q2(), _NT, preferred_element_type=F32)

    def weighted_values(st):
        v = vt2()
        n_v = v.shape[0]
        lhs = jnp.concatenate([v, jnp.ones((BF16_ROWS, v.shape[1]), BF16)], axis=0)
        sink_row = None
        if sinks[0] is not None:
            sink_row = jnp.concatenate([jnp.full((1, LANES), s_, F32) for s_ in sinks], axis=1)
        m = sink_row if sink_row is not None else jnp.full((1, 2 * LANES), NEG_INF, F32)
        acc = None
        for c in range(st.shape[0] // LANES):
            keys = slice(c * LANES, (c + 1) * LANES)
            s_c = st[keys, :]
            if biases[0] is not None:
                s_c = s_c + jnp.concatenate([b(keys) for b in biases], axis=1)
            m_new = jnp.maximum(m, jnp.max(s_c, axis=0, keepdims=True))
            e = jnp.exp2(s_c - m_new).astype(BF16)
            part = jnp.dot(lhs[:, keys], e, preferred_element_type=F32)
            acc = part if acc is None else acc * jnp.exp2(m - m_new) + part
            m = m_new
        l = acc[n_v:n_v + 1, :]
        if sink_row is not None:
            l = l + jnp.exp2(sink_row - m)
        inv = 1.0 / l
        outs = []
        for p in range(2):
            cols = slice(p * LANES, (p + 1) * LANES)
            r0 = p * second_row
            outs.append(acc[r0:r0 + HEAD_DIM, cols] * inv[:, cols])
        store(jnp.concatenate(outs, axis=0))

    return scores, weighted_values


def _mem_units(qm_ref, rows, mkv_ref, mvt_ref, masks, store_pair):
    def q2(t):
        qm = qm_ref[rows, :]
        return jnp.concatenate([qm * masks[2 * t], qm * masks[2 * t + 1]], axis=0)

    return [_pair_unit(lambda: mkv_ref[:, :MW], functools.partial(q2, t),
                       lambda t=t: mvt_ref[t * LANES:(t + 1) * LANES, :], HEAD_DIM,
                       (None, None), (None, None), functools.partial(store_pair, t))
            for t in range(N_MEM_HEADS // 2)]


def _mem_kv_kernel(mem_ref, g_ref, w_ref, w0_ref, o_ref, w0_out_ref, *, job):
    mem_n = _rms(mem_ref[...], g_ref[...])
    o_ref[...] = jnp.dot(mem_n.astype(BF16), w_ref[...].astype(BF16),
                         preferred_element_type=F32).astype(BF16)
    _cast_chunk(pl.program_id(0), job, w0_ref, w0_out_ref)


def _mem_kv(mem, norm_mem, w_mem_kv, w_in0, regroup0):
    depth, d, w = w_mem_kv.shape
    n_mem = mem.shape[0]
    job = _CastJob(w_in0, 0, depth, regroup0)
    return pl.pallas_call(
        functools.partial(_mem_kv_kernel, job=job),
        grid=(depth,),
        in_specs=[_resident((n_mem, d)), _resident((1, d)),
                  pl.BlockSpec((None, d, w), lambda i: (i, 0, 0)), job.in_spec],
        out_specs=[pl.BlockSpec((None, n_mem, w), lambda i: (i, 0, 0)), job.out_spec],
        out_shape=[jax.ShapeDtypeStruct((depth, n_mem, w), BF16), job.out_shape],
        name="mem_kv",
    )(mem, norm_mem.reshape(1, d), w_mem_kv, w_in0)


def _query_col_scale(width, n_q, n_qm):
    cs = np.ones((1, width), np.float32)
    cs[:, :n_q] = Q_SCALE
    cs[:, width - n_qm:] = Q_SCALE
    return jnp.asarray(cs)


def _proj_kernel(x_ref, g_ref, w_ref, cs_ref, o_ref):
    tiles = [slice(k * SUB_TILE, (k + 1) * SUB_TILE) for k in range(PROJ_TILE // SUB_TILE)]
    h = _rms(x_ref[tiles[0], :], g_ref[...]).astype(BF16)
    for k, rows in enumerate(tiles):
        h_next = None
        if k + 1 < len(tiles):
            h_next = _rms(x_ref[tiles[k + 1], :], g_ref[...]).astype(BF16)
        o_ref[rows, :] = (jnp.dot(h, w_ref[...], preferred_element_type=F32)
                          * cs_ref[...]).astype(BF16)
        h = h_next


def _proj(x, g, layer, w):
    s, d = x.shape
    n = w.shape[1]
    return pl.pallas_call(
        _proj_kernel,
        grid=(s // PROJ_TILE,),
        in_specs=[pl.BlockSpec((PROJ_TILE, d), lambda i: (i, 0)),
                  _layer_block(g, layer), _resident(w.shape), _resident((1, n))],
        out_specs=pl.BlockSpec((PROJ_TILE, n), lambda i: (i, 0)),
        out_shape=jax.ShapeDtypeStruct((s, n), BF16),
        compiler_params=pltpu.CompilerParams(
            dimension_semantics=("arbitrary",), vmem_limit_bytes=VMEM_LIMIT_BYTES),
        name="in_proj",
    )(x, g, w, _query_col_scale(n, QW, MW))


def _alibi_slope(h):
    return 2.0 ** (-8.0 * (h + 1) / N_Q_HEADS)


def _q_regroup_moves_a(width):
    heads = [GROUP_A * j + g for g in range(GROUP_A) for j in range(N_KV_HEADS_A)]
    moves = [(t * LANES, [heads[2 * t] * HEAD_DIM, heads[2 * t + 1] * HEAD_DIM], HEAD_DIM)
             for t in range(N_Q_HEADS // 2)]
    return moves + [(QW, [QW], width - QW)]


def _slab_block_a(i, nb):
    return jnp.clip(i * SUB_BLOCKS_A - 1, 0, nb - (SUB_BLOCKS_A + 2))


def _attn_a_kernel(sink_ref, q_ref, qm_ref, kv_ref, mkv_ref, *rest, nb, jobs):
    n = len(jobs)
    y_ref, bias_ref, mvt_ref, vbt_ref = rest[n], rest[2 * n + 1], rest[2 * n + 2], rest[2 * n + 3]
    i = pl.program_id(0)
    band = 3 * BLOCK_A
    for job, src_ref, dst_ref in zip(jobs, rest[:n], rest[n + 1:2 * n + 1]):
        _cast_chunk(i, job, src_ref, dst_ref)

    @pl.when(i == 0)
    def _():
        c = lax.broadcasted_iota(jnp.int32, (band, BLOCK_A), 0)
        r = lax.broadcasted_iota(jnp.int32, (band, BLOCK_A), 1)
        for var in range(3):
            dist = jnp.abs(r - c + var * BLOCK_A)
            allowed = dist <= WINDOW
            distf = dist.astype(F32)
            for h in range(N_Q_HEADS):
                bias_ref[var, h] = jnp.where(allowed, -(_alibi_slope(h) * LOG2E) * distf, NEG_INF)

        mvt_ref[...] = mkv_ref[:, MW:].T

    slab_blk = _slab_block_a(i, nb)
    masks = _slot_masks(BLOCK_A)

    def store_pair(rows, t, z):
        y_ref[rows, t * LANES:(t + 1) * LANES] = z.T.astype(BF16)

    def token_unit(s, t, off, var):
        rows = slice(s * BLOCK_A, (s + 1) * BLOCK_A)
        heads = (2 * t, 2 * t + 1)
        kvh = [h // GROUP_A for h in heads]

        def k_all():
            if t == 0:
                vbt_ref[s] = kv_ref[pl.ds(off, band), KVW_A:].T
            return kv_ref[pl.ds(off, band), :KVW_A]

        def q2():
            return jnp.concatenate(
                [q_ref[rows, (h % GROUP_A) * QUAD:(h % GROUP_A + 1) * QUAD] * masks[j]
                 for h, j in zip(heads, kvh)], axis=0)

        return _pair_unit(
            k_all, q2, lambda: vbt_ref[s, kvh[0] * HEAD_DIM:(kvh[1] + 1) * HEAD_DIM, :],
            (kvh[1] - kvh[0]) * HEAD_DIM, [sink_ref[h] * LOG2E for h in heads],
            [lambda keys, h=h: bias_ref[var, h, keys, :] for h in heads],
            functools.partial(store_pair, rows, t))

    units = []
    for s in range(SUB_BLOCKS_A):
        g_blk = i * SUB_BLOCKS_A + s
        band_blk = jnp.clip(g_blk - 1, 0, nb - 3)
        off = pl.multiple_of((band_blk - slab_blk) * BLOCK_A, BLOCK_A)
        var = g_blk - band_blk
        rows = slice(s * BLOCK_A, (s + 1) * BLOCK_A)
        units += [token_unit(s, t, off, var) for t in range(N_Q_HEADS // 2)]
        units += _mem_units(qm_ref, rows, mkv_ref, mvt_ref, masks,
                            lambda t, z, rows=rows: store_pair(rows, N_Q_HEADS // 2 + t, z))
    _run_pipelined(units, STAGE_DELAY_A)


def _attn_a(proj, mkv, layer, sink, casts):
    seq, _ = proj.shape
    nb = seq // BLOCK_A
    tq = SUB_BLOCKS_A * BLOCK_A
    slab = (SUB_BLOCKS_A + 2) * BLOCK_A
    n_steps = nb // SUB_BLOCKS_A
    jobs = [_CastJob(w, l, n_steps, rg) for w, l, rg in casts]
    outs = pl.pallas_call(
        functools.partial(_attn_a_kernel, nb=nb, jobs=jobs),
        grid=(n_steps,),
        in_specs=[
            pl.BlockSpec(memory_space=pltpu.SMEM),
            pl.BlockSpec((tq, QW), lambda i: (i, 0)),
            pl.BlockSpec((tq, MW), lambda i: (i, (QW + 2 * KVW_A) // MW)),
            pl.BlockSpec((pl.Element(slab), pl.Element(2 * KVW_A)),
                         lambda i: (_slab_block_a(i, nb) * BLOCK_A, QW)),
            _layer_block(mkv, layer),
        ] + [job.in_spec for job in jobs],
        out_specs=[pl.BlockSpec((tq, QW + MW), lambda i: (i, 0))] + [job.out_spec for job in jobs],
        out_shape=[jax.ShapeDtypeStruct((seq, QW + MW), BF16)] + [job.out_shape for job in jobs],
        scratch_shapes=[pltpu.VMEM((3, N_Q_HEADS, 3 * BLOCK_A, BLOCK_A), F32),
                        pltpu.VMEM((MW, mkv.shape[1]), BF16),
                        pltpu.VMEM((SUB_BLOCKS_A, KVW_A, 3 * BLOCK_A), BF16)],
        compiler_params=pltpu.CompilerParams(
            dimension_semantics=("arbitrary",), vmem_limit_bytes=VMEM_LIMIT_BYTES),
        name="attn_a",
    )(sink, proj, proj, proj, mkv, *[w for w, _, _ in casts])
    return outs[0], outs[1:]


N_DR = 2 * NA_ROWS - 1
N_DC = 2 * NA_COLS - 1
N_QUADS = N_Q_HEADS // 4


def _bias_table_kernel(rpb_ref, o_ref):
    li = pl.program_id(0)
    hq = pl.program_id(1)
    kc = lax.broadcasted_iota(jnp.int32, (GRID_W, QUAD), 0)
    lane = lax.broadcasted_iota(jnp.int32, (GRID_W, QUAD), 1)
    qc = lane % GRID_W
    col_start = jnp.clip(qc - NA_COLS // 2, 0, GRID_W - NA_COLS)
    in_win = (kc >= col_start) & (kc < col_start + NA_COLS)
    idx = jnp.clip(kc - qc, -(NA_COLS - 1), NA_COLS - 1) + NA_COLS - 1
    slot_row = _head_slot((1, QUAD))
    base = (li * N_Q_HEADS + 4 * hq) * (N_DR * N_DC)

    def body(d, carry):
        acc = jnp.zeros((GRID_W, QUAD), F32)
        for v in range(N_DC):
            a = [rpb_ref[base + hl * (N_DR * N_DC) + d * N_DC + v] for hl in range(4)]
            val = jnp.where(slot_row == 0, a[0],
                            jnp.where(slot_row == 1, a[1], jnp.where(slot_row == 2, a[2], a[3])))
            acc = jnp.where(idx == v, val, acc)
        o_ref[pl.ds(pl.multiple_of(d * GRID_W, GRID_W), GRID_W), :] = jnp.where(in_win, acc * LOG2E, NEG_INF)
        return carry

    lax.fori_loop(0, N_DR, body, 0)


def _bias_tables(rpb_b):
    n_layers = rpb_b.shape[0]
    return pl.pallas_call(
        _bias_table_kernel,
        grid=(n_layers, N_QUADS),
        in_specs=[pl.BlockSpec(memory_space=pltpu.SMEM)],
        out_specs=pl.BlockSpec((None, None, N_DR * GRID_W, QUAD),
                               lambda li, hq: (li, hq, 0, 0)),
        out_shape=jax.ShapeDtypeStruct((n_layers, N_QUADS, N_DR * GRID_W, QUAD), F32),
        name="rpb_table",
    )(rpb_b.reshape(-1))


SLAB_ROWS_B = ROWS_PER_STEP_B + NA_ROWS


def _slab_start_row(b, rows):
    return jnp.clip(b * ROWS_PER_STEP_B - NA_ROWS // 2, 0, rows - SLAB_ROWS_B)


def _attn_b_kernel(q_ref, qm_ref, kv_ref, bias_ref, mkv_ref, *rest, rows, jobs):
    n = len(jobs)
    y_ref, mvt_ref = rest[n], rest[2 * n + 1]
    b = pl.program_id(0)
    for job, src_ref, dst_ref in zip(jobs, rest[:n], rest[n + 1:2 * n + 1]):
        _cast_chunk(b, job, src_ref, dst_ref)
    win = NA_ROWS * GRID_W
    slot = _head_slot((GRID_W, QUAD))
    slab0 = _slab_start_row(b, rows)

    @pl.when(b == 0)
    def _():
        mvt_ref[...] = mkv_ref[:, MW:].T

    masks64 = _slot_masks(GRID_W)
    masks128 = _slot_masks(LANES)

    def quad_unit(j, hq):
        r = b * ROWS_PER_STEP_B + j
        row_start = jnp.clip(r - NA_ROWS // 2, 0, rows - NA_ROWS)
        koff = pl.multiple_of((row_start - slab0) * GRID_W, GRID_W)
        boff = pl.multiple_of((row_start - r + NA_ROWS - 1) * GRID_W, GRID_W)
        qrows = slice(j * GRID_W, (j + 1) * GRID_W)
        lo = hq * QUAD

        def scores(_):
            q4 = q_ref[qrows, lo:lo + QUAD]
            qd = jnp.concatenate([q4 * masks64[hl] for hl in range(4)], axis=0)
            k4 = kv_ref[pl.ds(koff, win), lo:lo + QUAD]
            st = lax.dot_general(k4, qd, _NT, preferred_element_type=F32)
            st = st + bias_ref[hq, pl.ds(boff, win), :]
            return st, jnp.max(st, axis=0, keepdims=True)

        def softmax(state):
            st, m = state
            e = jnp.exp2(st - m)
            return e.astype(BF16), jnp.sum(e, axis=0, keepdims=True)

        def weighted_values(state):
            p, l = state
            v4 = kv_ref[pl.ds(koff, win), QW + lo:QW + lo + QUAD]
            o2 = lax.dot_general(p, v4, _TN, preferred_element_type=F32)
            inv = jnp.broadcast_to(1.0 / l, (LANES, QUAD)).T
            y4 = o2[:GRID_W] * jnp.tile(inv[:GRID_W], (1, 2))
            for hl in range(1, 4):
                part = o2[hl * GRID_W:(hl + 1) * GRID_W] * jnp.tile(inv[hl * GRID_W:(hl + 1) * GRID_W], (1, 2))
                y4 = jnp.where(slot == hl, part, y4)
            y_ref[qrows, lo:lo + QUAD] = y4.astype(BF16)

        return scores, softmax, weighted_values

    def store_mem_pair(blk, t, z):
        y_ref[blk, QW + t * LANES:QW + (t + 1) * LANES] = z.T.astype(BF16)

    units = [quad_unit(j, hq) for j in range(ROWS_PER_STEP_B) for hq in range(N_QUADS)]
    for s in range(ROWS_PER_STEP_B * GRID_W // LANES):
        blk = slice(s * LANES, (s + 1) * LANES)
        units += _mem_units(qm_ref, blk, mkv_ref, mvt_ref, masks128,
                            functools.partial(store_mem_pair, blk))
    _run_pipelined(units, STAGE_DELAY_B)


def _attn_b(proj, mkv, layer, bias, bias_layer, casts):
    seq, _ = proj.shape
    rows = seq // GRID_W
    tq = ROWS_PER_STEP_B * GRID_W
    slab = SLAB_ROWS_B * GRID_W
    n_steps = rows // ROWS_PER_STEP_B
    jobs = [_CastJob(w, l, n_steps, rg) for w, l, rg in casts]
    outs = pl.pallas_call(
        functools.partial(_attn_b_kernel, rows=rows, jobs=jobs),
        grid=(n_steps,),
        in_specs=[
            pl.BlockSpec((tq, QW), lambda b: (b, 0)),
            pl.BlockSpec((tq, MW), lambda b: (b, 3 * QW // MW)),
            pl.BlockSpec((pl.Element(slab), pl.Element(2 * QW)),
                         lambda b: (_slab_start_row(b, rows) * GRID_W, QW)),
            _layer_block(bias, bias_layer),
            _layer_block(mkv, layer),
        ] + [job.in_spec for job in jobs],
        out_specs=[pl.BlockSpec((tq, QW + MW), lambda b: (b, 0))] + [job.out_spec for job in jobs],
        out_shape=[jax.ShapeDtypeStruct((seq, QW + MW), BF16)] + [job.out_shape for job in jobs],
        scratch_shapes=[pltpu.VMEM((MW, mkv.shape[1]), BF16)],
        compiler_params=pltpu.CompilerParams(
            dimension_semantics=("arbitrary",), vmem_limit_bytes=VMEM_LIMIT_BYTES),
        name="attn_b",
    )(proj, proj, proj, bias, mkv, *[w for w, _, _ in casts])
    return outs[0], outs[1:]


def _post_kernel(x_ref, y_ref, wo_ref, gf_ref, wgu_ref, wd_ref, gn_ref, *rest, d_ff, last):
    tiles = [slice(k * SUB_TILE, (k + 1) * SUB_TILE) for k in range(x_ref.shape[0] // SUB_TILE)]
    x1 = [x_ref[rows, :] + jnp.dot(y_ref[rows, :], wo_ref[...], preferred_element_type=F32)
          for rows in tiles]
    act = []
    for k in range(len(tiles)):
        h = _rms(x1[k], gf_ref[...]).astype(BF16)
        gu = jnp.dot(h, wgu_ref[...], preferred_element_type=F32)
        g = gu[:, :d_ff]
        u = gu[:, d_ff:]
        act.append((g * (1.0 / (1.0 + jnp.exp(-g))) * u).astype(BF16))
    x2 = [x1[k] + jnp.dot(act[k], wd_ref[...], preferred_element_type=F32)
          for k in range(len(tiles))]
    if last:
        (o_ref,) = rest
        for k, rows in enumerate(tiles):
            o_ref[rows, :] = _rms(x2[k], gn_ref[...])
    else:
        wn_ref, cs_ref, o_ref, p_ref = rest
        for k, rows in enumerate(tiles):
            o_ref[rows, :] = x2[k]
            hn = _rms(x2[k], gn_ref[...]).astype(BF16)
            p_ref[rows, :] = (jnp.dot(hn, wn_ref[...], preferred_element_type=F32)
                              * cs_ref[...]).astype(BF16)


def _post(x, y, w_out, g_ffn, layer, w_gu, w_d, g_next, next_layer, w_next):
    s, d = x.shape
    d_ff = w_d.shape[0]
    last = w_next is None
    row_tile = LAST_ROW_TILE if last else ROW_TILE
    row = lambda i: (i, 0)
    in_specs = [pl.BlockSpec((row_tile, d), row), pl.BlockSpec((row_tile, y.shape[1]), row),
                _resident(w_out.shape), _layer_block(g_ffn, layer), _resident(w_gu.shape),
                _resident(w_d.shape), _layer_block(g_next, next_layer)]
    args = [x, y, w_out, g_ffn, w_gu, w_d, g_next]
    out_specs = [pl.BlockSpec((row_tile, d), row)]
    out_shape = [jax.ShapeDtypeStruct((s, d), F32)]
    if not last:
        n = w_next.shape[1]
        in_specs += [_resident(w_next.shape), _resident((1, n))]
        args += [w_next, _query_col_scale(n, QW, MW)]
        out_specs.append(pl.BlockSpec((row_tile, n), row))
        out_shape.append(jax.ShapeDtypeStruct((s, n), BF16))
    outs = pl.pallas_call(
        functools.partial(_post_kernel, d_ff=d_ff, last=last),
        grid=(s // row_tile,),
        in_specs=in_specs,
        out_specs=out_specs,
        out_shape=out_shape,
        compiler_params=pltpu.CompilerParams(
            dimension_semantics=("arbitrary",), vmem_limit_bytes=VMEM_LIMIT_BYTES),
        name="post_last" if last else "post",
    )(*args)
    return (outs[0], None) if last else (outs[0], outs[1])


def kernel(x, mem, norm_mix, norm_ffn, norm_mem, norm_final, w_in_a, sink_a, w_in_b, rpb_b,
           w_mem_kv, w_out, w_gate_up, w_down):
    batch, seq, d = x.shape
    assert batch == 1 and mem.shape[0] == 1
    assert seq % (ROWS_PER_STEP_B * GRID_W) == 0 and seq % ROW_TILE == 0 and seq % PROJ_TILE == 0 and seq % LAST_ROW_TILE == 0
    assert seq % (SUB_BLOCKS_A * BLOCK_A) == 0 and seq // BLOCK_A >= SUB_BLOCKS_A + 2
    depth = w_out.shape[0]
    xs = x.reshape(seq, d)
    w_in = (w_in_a, w_in_b)
    regroup = (_q_regroup_moves_a(w_in_a.shape[2]), None)
    g_mix = norm_mix.reshape(depth, 1, d)
    g_ffn = norm_ffn.reshape(depth, 1, d)
    g_final = norm_final.reshape(1, 1, d)

    mkv, w_in0 = _mem_kv(mem.reshape(mem.shape[1], d), norm_mem, w_mem_kv, w_in[0], regroup[0])
    bias = _bias_tables(rpb_b)

    proj = _proj(xs, g_mix, 0, w_in0)
    for i in range(depth):
        last = i == depth - 1
        nxt = (i + 1) % 2
        casts = [(w_out, i, None), (w_gate_up, i, None), (w_down, i, None)]
        if not last:
            casts.append((w_in[nxt], (i + 1) // 2, regroup[nxt]))
        if i % 2 == 0:
            y, wb = _attn_a(proj, mkv, i, sink_a[i // 2], casts)
        else:
            y, wb = _attn_b(proj, mkv, i, bias, i // 2, casts)
        if last:
            xs, proj = _post(xs, y, wb[0], g_ffn, i, wb[1], wb[2], g_final, 0, None)
        else:
            xs, proj = _post(xs, y, wb[0], g_ffn, i, wb[1], wb[2], g_mix, i + 1, wb[3])
    return xs.reshape(batch, seq, d)
```
